```python
import jax, jax.numpy as jnp
from jax import lax
import numpy as np

D_MODEL = 1024
BATCH = 8
SEQ = 2048
DEPTH = 4

CTX_LEN = 256
GRID_W = 64
N_MIXERS = 2
N_HEADS = 16
N_KV_HEADS = 4
HEAD_DIM = D_MODEL // N_HEADS
KV_GROUP = N_HEADS // N_KV_HEADS
ROPE_PAIRS_PER_AXIS = HEAD_DIM // 4
ROPE_THETA = 10000.0
Q_BLOCK = 128
CONV_WIDTH = 31
CONV_PAD = CONV_WIDTH // 2
D_FF = (((8 * D_MODEL + 2) // 3 + 255) // 256) * 256
N_CONV_LAYERS = (DEPTH + N_MIXERS - 1) // N_MIXERS
N_ATTN_LAYERS = DEPTH // N_MIXERS
N_MOD = 6
EPS = 1e-6

kernel_name = "hybrid_conv_gqa_diffusion_trunk"


def rmsnorm(x, g):
    xf = x.astype(jnp.float32)
    y = xf * lax.rsqrt(jnp.mean(xf * xf, axis=-1, keepdims=True) + EPS)
    return (y * g.astype(jnp.float32)).astype(x.dtype)


def layernorm(x, g, b):
    xf = x.astype(jnp.float32)
    mu = jnp.mean(xf, axis=-1, keepdims=True)
    var = jnp.mean(jnp.square(xf - mu), axis=-1, keepdims=True)
    y = (xf - mu) * lax.rsqrt(var + EPS)
    return (y * g.astype(jnp.float32) + b.astype(jnp.float32)).astype(x.dtype)


def axial_rope_tables(row, col):
    freqs = ROPE_THETA ** (-jnp.arange(ROPE_PAIRS_PER_AXIS, dtype=jnp.float32) / ROPE_PAIRS_PER_AXIS)
    ang = jnp.concatenate([row.astype(jnp.float32)[:, None] * freqs[None, :],
                           col.astype(jnp.float32)[:, None] * freqs[None, :]], axis=-1)
    return jnp.cos(ang), jnp.sin(ang)


def apply_rope(x, cos, sin):
    B, S, H, Dh = x.shape
    xr = x.astype(jnp.float32).reshape(B, S, H, Dh // 2, 2)
    x0, x1 = xr[..., 0], xr[..., 1]
    c = cos[None, :, None, :]
    s = sin[None, :, None, :]
    out = jnp.stack([x0 * c - x1 * s, x0 * s + x1 * c], axis=-1)
    return out.reshape(B, S, H, Dh).astype(x.dtype)


def conv_module(h, w_pw1, b_pw1, w_dw, b_dw, ln_g, ln_b, w_pw2, b_pw2):
    u = h @ w_pw1 + b_pw1
    a, gt = jnp.split(u, 2, axis=-1)
    u = a * jax.nn.sigmoid(gt)
    u = lax.conv_general_dilated(
        u, w_dw[:, None, :].astype(u.dtype), window_strides=(1,),
        padding=[(CONV_PAD, CONV_PAD)], dimension_numbers=("NWC", "WIO", "NWC"),
        feature_group_count=u.shape[-1]) + b_dw
    u = jax.nn.silu(layernorm(u, ln_g, ln_b))
    return u @ w_pw2 + b_pw2


def gqa_sdpa(q, k, v):
    B, Q, H, Dh = q.shape
    qg = q.reshape(B, Q, N_KV_HEADS, KV_GROUP, Dh)
    s = jnp.einsum("bqkgd,bskd->bkgqs", qg, k).astype(jnp.float32) * (Dh ** -0.5)
    p = jax.nn.softmax(s, axis=-1).astype(v.dtype)
    o = jnp.einsum("bkgqs,bskd->bqkgd", p, v)
    return o.reshape(B, Q, H * Dh)


def attention_module(hx, hc, wq, wk, wv, wo, q_g, k_g, cos, sin, need_ctx):
    B, S, _ = hx.shape
    L = hc.shape[1]
    qx = apply_rope(rmsnorm((hx @ wq).reshape(B, S, N_HEADS, HEAD_DIM), q_g), cos, sin)
    kx = apply_rope(rmsnorm((hx @ wk).reshape(B, S, N_KV_HEADS, HEAD_DIM), k_g), cos, sin)
    vx = (hx @ wv).reshape(B, S, N_KV_HEADS, HEAD_DIM)
    kc = rmsnorm((hc @ wk).reshape(B, L, N_KV_HEADS, HEAD_DIM), k_g)
    vc = (hc @ wv).reshape(B, L, N_KV_HEADS, HEAD_DIM)
    k_all = jnp.concatenate([kx, kc], axis=1)
    v_all = jnp.concatenate([vx, vc], axis=1)
    n_blk = S // Q_BLOCK
    qb = qx.reshape(B, n_blk, Q_BLOCK, N_HEADS, HEAD_DIM).transpose(1, 0, 2, 3, 4)
    ob = lax.map(lambda q: gqa_sdpa(q, k_all, v_all), qb)
    yx = ob.transpose(1, 0, 2, 3).reshape(B, S, N_HEADS * HEAD_DIM) @ wo
    yc = None
    if need_ctx:
        qc = rmsnorm((hc @ wq).reshape(B, L, N_HEADS, HEAD_DIM), q_g)
        yc = gqa_sdpa(qc, kc, vc) @ wo
    return yx, yc


def swiglu(h, w1, w3, w2):
    return (jax.nn.silu(h @ w1) * (h @ w3)) @ w2


def setup_inputs(seed: int = 0) -> dict:
    key = jax.random.key(seed)
    ks = jax.random.split(key, 32)
    D = D_MODEL

    def nrm(k, shape, scale):
        return jax.random.normal(k, shape, jnp.float32) * scale

    NC, NA = N_CONV_LAYERS, N_ATTN_LAYERS
    return {
        "x": nrm(ks[0], (BATCH, SEQ, D), 1.0),
        "c": nrm(ks[1], (BATCH, D), 1.0),
        "ctx": nrm(ks[2], (BATCH, CTX_LEN, D), 1.0),
        "c_ctx": nrm(ks[3], (D,), 1.0),
        "w_mod": nrm(ks[4], (DEPTH, D, N_MOD * D), 0.5 * D ** -0.5),
        "b_mod": nrm(ks[5], (DEPTH, N_MOD * D), 0.01),
        "norm_g": 1.0 + nrm(ks[6], (DEPTH, 4, D), 0.05),
        "conv_w_pw1": nrm(ks[7], (NC, D, 2 * D), D ** -0.5),
        "conv_b_pw1": nrm(ks[8], (NC, 2 * D), 0.01),
        "conv_w_dw": nrm(ks[9], (NC, CONV_WIDTH, D), CONV_WIDTH ** -0.5),
        "conv_b_dw": nrm(ks[10], (NC, D), 0.01),
        "conv_ln_g": 1.0 + nrm(ks[11], (NC, D), 0.05),
        "conv_ln_b": nrm(ks[12], (NC, D), 0.01),
        "conv_w_pw2": nrm(ks[13], (NC, D, D), D ** -0.5),
        "conv_b_pw2": nrm(ks[14], (NC, D), 0.01),
        "attn_wq": nrm(ks[15], (NA, D, N_HEADS * HEAD_DIM), D ** -0.5),
        "attn_wk": nrm(ks[16], (NA, D, N_KV_HEADS * HEAD_DIM), D ** -0.5),
        "attn_wv": nrm(ks[17], (NA, D, N_KV_HEADS * HEAD_DIM), D ** -0.5),
        "attn_wo": nrm(ks[18], (NA, N_HEADS * HEAD_DIM, D), (N_HEADS * HEAD_DIM) ** -0.5),
        "attn_q_g": 1.0 + nrm(ks[19], (NA, HEAD_DIM), 0.05),
        "attn_k_g": 1.0 + nrm(ks[20], (NA, HEAD_DIM), 0.05),
        "ffn_w1": nrm(ks[21], (DEPTH, D, D_FF), D ** -0.5),
        "ffn_w3": nrm(ks[22], (DEPTH, D, D_FF), D ** -0.5),
        "ffn_w2": nrm(ks[23], (DEPTH, D_FF, D), D_FF ** -0.5),
    }


def reference(x, c, ctx, c_ctx, w_mod, b_mod, norm_g,
              conv_w_pw1, conv_b_pw1, conv_w_dw, conv_b_dw, conv_ln_g, conv_ln_b, conv_w_pw2, conv_b_pw2,
              attn_wq, attn_wk, attn_wv, attn_wo, attn_q_g, attn_k_g,
              ffn_w1, ffn_w3, ffn_w2):
    B, S, D = x.shape
    rows = S // GRID_W
    row = jnp.repeat(jnp.arange(rows), GRID_W)
    col = jnp.tile(jnp.arange(GRID_W), rows)
    cos, sin = axial_rope_tables(row, col)
    silu_c = jax.nn.silu(c)
    silu_cc = jax.nn.silu(c_ctx)

    for i in range(DEPTH):
        need_ctx = i < DEPTH - 1
        mod_x = (silu_c @ w_mod[i] + b_mod[i])[:, None, :]
        mod_c = silu_cc @ w_mod[i] + b_mod[i]
        sh_x, sc_x, g_x, shf_x, scf_x, gf_x = jnp.split(mod_x, N_MOD, axis=-1)
        sh_c, sc_c, g_c, shf_c, scf_c, gf_c = jnp.split(mod_c, N_MOD, axis=-1)

        hx = rmsnorm(x, norm_g[i, 0]) * (1.0 + sc_x) + sh_x
        hc = rmsnorm(ctx, norm_g[i, 0]) * (1.0 + sc_c) + sh_c
        if i % N_MIXERS == 0:
            j = i // N_MIXERS
            cp = (conv_w_pw1[j], conv_b_pw1[j], conv_w_dw[j], conv_b_dw[j],
                  conv_ln_g[j], conv_ln_b[j], conv_w_pw2[j], conv_b_pw2[j])
            yx = conv_module(hx, *cp)
            yc = conv_module(hc, *cp) if need_ctx else None
        else:
            j = i // N_MIXERS
            yx, yc = attention_module(hx, hc, attn_wq[j], attn_wk[j], attn_wv[j], attn_wo[j],
                                      attn_q_g[j], attn_k_g[j], cos, sin, need_ctx)
        x = x + g_x * rmsnorm(yx, norm_g[i, 1])
        if need_ctx:
            ctx = ctx + g_c * rmsnorm(yc, norm_g[i, 1])

        fx = rmsnorm(x, norm_g[i, 2]) * (1.0 + scf_x) + shf_x
        x = x + gf_x * rmsnorm(swiglu(fx, ffn_w1[i], ffn_w3[i], ffn_w2[i]), norm_g[i, 3])
        if need_ctx:
            fc = rmsnorm(ctx, norm_g[i, 2]) * (1.0 + scf_c) + shf_c
            ctx = ctx + gf_c * rmsnorm(swiglu(fc, ffn_w1[i], ffn_w3[i], ffn_w2[i]), norm_g[i, 3])
    return x
```

```python
import functools

import numpy as np
import jax
import jax.numpy as jnp
from jax import lax
from jax.experimental import pallas as pl
from jax.experimental.pallas import tpu as pltpu

D_MODEL = 1024
BATCH = 8
SEQ = 2048
DEPTH = 4
CTX_LEN = 256
GRID_W = 64
N_HEADS = 16
N_KV_HEADS = 4
HEAD_DIM = 64
KV_GROUP = N_HEADS // N_KV_HEADS
ROPE_PAIRS_PER_AXIS = HEAD_DIM // 4
ROPE_THETA = 10000.0
CONV_WIDTH = 31
CONV_PAD = CONV_WIDTH // 2
D_FF = 2816
N_MOD = 6
EPS = 1e-6

N_STACK = BATCH + 1
CTX_BLK = BATCH
KV_DIM = N_KV_HEADS * HEAD_DIM
QKV_DIM = D_MODEL + 2 * KV_DIM
LANES = 128
HALO = 16
MOD_ROWS = 16
VMEM_LIMIT = 56 * 1024 * 1024

BF16 = jnp.bfloat16
F32 = jnp.float32


def _params(sem):
    return pltpu.CompilerParams(dimension_semantics=sem, vmem_limit_bytes=VMEM_LIMIT)


def _rms(x, g):
    return x * lax.rsqrt(jnp.mean(x * x, axis=-1, keepdims=True) + EPS) * g


def _sigmoid(x):
    return 1.0 / (1.0 + jnp.exp(-x))


def _resident(shape):
    nd = len(shape)
    return pl.BlockSpec(shape, lambda *_: (0,) * nd, pipeline_mode=pl.Buffered(1))


def _small(shape):
    nd = len(shape)
    return pl.BlockSpec(shape, lambda *_: (0,) * nd)


def _mod_kernel(cc_ref, w_ref, b_ref, o_ref):
    s = cc_ref[...]
    s = s * _sigmoid(s)
    o_ref[0] = jnp.dot(s.astype(BF16), w_ref[0].astype(BF16),
                       preferred_element_type=F32) + b_ref[0]


def _mod_table(cc, w_mod, b_mod):
    tn = 1536
    n_out = N_MOD * D_MODEL
    return pl.pallas_call(
        _mod_kernel,
        grid=(DEPTH, n_out // tn),
        in_specs=[
            pl.BlockSpec((MOD_ROWS, D_MODEL), lambda i, j: (0, 0)),
            pl.BlockSpec((1, D_MODEL, tn), lambda i, j: (i, 0, j)),
            pl.BlockSpec((1, 1, tn), lambda i, j: (i, 0, j)),
        ],
        out_specs=pl.BlockSpec((1, MOD_ROWS, tn), lambda i, j: (i, 0, j)),
        out_shape=jax.ShapeDtypeStruct((DEPTH, MOD_ROWS, n_out), F32),
        compiler_params=_params(("arbitrary", "arbitrary")),
        name="mod_table",
    )(cc, w_mod, b_mod.reshape(DEPTH, 1, n_out))


def _ffn_kernel(x_ref, mod_ref, g_ref, w1_ref, w3_ref, w2_ref, o_ref, *, n_chunks):
    x = x_ref[0]
    h = _rms(x, g_ref[2:3, :]) * (1.0 + mod_ref[0, 4:5, :]) + mod_ref[0, 3:4, :]
    hb = h.astype(BF16)
    fc = D_FF // n_chunks
    acc = None
    for c in range(n_chunks):
        a = jnp.dot(hb, w1_ref[:, c * fc:(c + 1) * fc], preferred_element_type=F32)
        b = jnp.dot(hb, w3_ref[:, c * fc:(c + 1) * fc], preferred_element_type=F32)
        act = (a * _sigmoid(a) * b).astype(BF16)
        part = jnp.dot(act, w2_ref[c * fc:(c + 1) * fc, :], preferred_element_type=F32)
        acc = part if acc is None else acc + part
    o_ref[0] = x + mod_ref[0, 5:6, :] * _rms(acc, g_ref[3:4, :])


def _ffn(xs, mod_i, g_i, w1, w3, w2, n_blk, in_place):
    tm = 512
    kern = functools.partial(_ffn_kernel, n_chunks=2)
    out_rows = N_STACK if in_place else n_blk
    return pl.pallas_call(
        kern,
        grid=(n_blk, SEQ // tm),
        in_specs=[
            pl.BlockSpec((1, tm, D_MODEL), lambda b, j: (b, j, 0)),
            pl.BlockSpec((1, N_MOD, D_MODEL), lambda b, j: (b, 0, 0)),
            _small((4, D_MODEL)),
            _resident((D_MODEL, D_FF)),
            _resident((D_MODEL, D_FF)),
            _resident((D_FF, D_MODEL)),
        ],
        out_specs=pl.BlockSpec((1, tm, D_MODEL), lambda b, j: (b, j, 0)),
        out_shape=jax.ShapeDtypeStruct((out_rows, SEQ, D_MODEL), F32),
        input_output_aliases={0: 0} if in_place else {},
        compiler_params=_params(("arbitrary", "arbitrary")),
        name="ffn",
    )(xs, mod_i, g_i, w1, w3, w2)


def _pw1_kernel(x_ref, mod_ref, g_ref, w_ref, b_ref, u_ref):
    x = x_ref[0]
    h = _rms(x, g_ref[0:1, :]) * (1.0 + mod_ref[0, 1:2, :]) + mod_ref[0, 0:1, :]
    hb = h.astype(BF16)
    a = jnp.dot(hb, w_ref[:, :D_MODEL], preferred_element_type=F32) + b_ref[:, :D_MODEL]
    gt = jnp.dot(hb, w_ref[:, D_MODEL:], preferred_element_type=F32) + b_ref[:, D_MODEL:]
    u_ref[0] = a * _sigmoid(gt)


def _pw1(xs, mod_i, g_i, w, b, n_blk):
    tm = 512
    return pl.pallas_call(
        _pw1_kernel,
        grid=(n_blk, SEQ // tm),
        in_specs=[
            pl.BlockSpec((1, tm, D_MODEL), lambda b_, j: (b_, j, 0)),
            pl.BlockSpec((1, N_MOD, D_MODEL), lambda b_, j: (b_, 0, 0)),
            _small((4, D_MODEL)),
            _resident((D_MODEL, 2 * D_MODEL)),
            _small((1, 2 * D_MODEL)),
        ],
        out_specs=pl.BlockSpec((1, tm, D_MODEL), lambda b_, j: (b_, j, 0)),
        out_shape=jax.ShapeDtypeStruct((N_STACK, SEQ, D_MODEL), F32),
        compiler_params=_params(("arbitrary", "arbitrary")),
        name="conv_pw1",
    )(xs, mod_i, g_i, w, b)


CONV_ROWS = 128


def _conv2_kernel(u_ref, up_ref, un_ref, x_ref, mod_ref, g_ref, wdw_ref, bdw_ref,
                  lng_ref, lnb_ref, w2_ref, b2_ref, o_ref, buf_ref, cv_ref, *, tm):
    j = pl.program_id(1)
    nj = pl.num_programs(1)
    base = HALO - CONV_PAD
    for c in range(D_MODEL // LANES):
        lanes = slice(c * LANES, (c + 1) * LANES)
        buf_ref[c, 0:HALO, :] = jnp.where(j > 0, up_ref[0, :, lanes], 0.0)
        buf_ref[c, HALO:HALO + tm, :] = u_ref[0, :, lanes]
        buf_ref[c, HALO + tm:HALO + tm + HALO, :] = jnp.where(j < nj - 1, un_ref[0, :, lanes], 0.0)

        def body(r, carry, c=c, lanes=lanes):
            r0 = pl.multiple_of(r * CONV_ROWS, CONV_ROWS)
            acc = jnp.zeros((CONV_ROWS, LANES), F32)
            for k in range(CONV_WIDTH):
                acc = acc + buf_ref[c, pl.ds(r0 + base + k, CONV_ROWS), :] * wdw_ref[k:k + 1, lanes]
            cv_ref[pl.ds(r0, CONV_ROWS), lanes] = acc + bdw_ref[:, lanes]
            return carry

        lax.fori_loop(0, tm // CONV_ROWS, body, 0)

    v = cv_ref[...]
    mu = jnp.mean(v, axis=-1, keepdims=True)
    vc = v - mu
    var = jnp.mean(vc * vc, axis=-1, keepdims=True)
    y = vc * lax.rsqrt(var + EPS) * lng_ref[...] + lnb_ref[...]
    y = y * _sigmoid(y)
    z = jnp.dot(y.astype(BF16), w2_ref[...], preferred_element_type=F32) + b2_ref[...]
    o_ref[0] = x_ref[0] + mod_ref[0, 2:3, :] * _rms(z, g_ref[1:2, :])


def _conv2(u, xs, mod_i, g_i, wdw, bdw, lng, lnb, w2, b2, *, ctx):
    if ctx:
        tm = CTX_LEN
        grid = (BATCH, 1)
        tile = lambda b, j: (CTX_BLK, b, 0)
        prev = lambda b, j: (CTX_BLK, 0, 0)
        nxt = lambda b, j: (CTX_BLK, 0, 0)
        mod_map = lambda b, j: (CTX_BLK, 0, 0)
    else:
        tm = 512
        grid = (BATCH, SEQ // tm)
        hb = tm // HALO
        tile = lambda b, j: (b, j, 0)
        prev = lambda b, j: (b, jnp.maximum(j * hb - 1, 0), 0)
        nxt = lambda b, j: (b, jnp.minimum((j + 1) * hb, SEQ // HALO - 1), 0)
        mod_map = lambda b, j: (b, 0, 0)
    kern = functools.partial(_conv2_kernel, tm=tm)
    return pl.pallas_call(
        kern,
        grid=grid,
        in_specs=[
            pl.BlockSpec((1, tm, D_MODEL), tile),
            pl.BlockSpec((1, HALO, D_MODEL), prev),
            pl.BlockSpec((1, HALO, D_MODEL), nxt),
            pl.BlockSpec((1, tm, D_MODEL), tile),
            pl.BlockSpec((1, N_MOD, D_MODEL), mod_map),
            _small((4, D_MODEL)),
            _small((CONV_WIDTH, D_MODEL)),
            _small((1, D_MODEL)),
            _small((1, D_MODEL)),
            _small((1, D_MODEL)),
            _resident((D_MODEL, D_MODEL)),
            _small((1, D_MODEL)),
        ],
        out_specs=pl.BlockSpec((1, tm, D_MODEL), tile),
        out_shape=jax.ShapeDtypeStruct((N_STACK, SEQ, D_MODEL), F32),
        scratch_shapes=[
            pltpu.VMEM((D_MODEL // LANES, tm + 2 * HALO, LANES), F32),
            pltpu.VMEM((tm, D_MODEL), F32),
        ],
        input_output_aliases={3: 0},
        compiler_params=_params(("arbitrary", "arbitrary")),
        name="conv_dw_pw2_ctx" if ctx else "conv_dw_pw2",
    )(u, u, u, xs, mod_i, g_i, wdw, bdw, lng, lnb, w2, b2)


def _head_norm_rope(x, g128, cos, sin_signed):
    lane = lax.broadcasted_iota(jnp.int32, x.shape, 1)
    first = lane < HEAD_DIM
    x2 = x * x
    ss_a = jnp.sum(jnp.where(first, x2, 0.0), axis=-1, keepdims=True)
    ss_b = jnp.sum(jnp.where(first, 0.0, x2), axis=-1, keepdims=True)
    ms = jnp.where(first, ss_a, ss_b) * (1.0 / HEAD_DIM)
    xn = x * lax.rsqrt(ms + EPS) * g128
    partner = jnp.where(lane % 2 == 0,
                        pltpu.roll(xn, LANES - 1, axis=1),
                        pltpu.roll(xn, 1, axis=1))
    return xn * cos + partner * sin_signed


def _qkv_kernel(x_ref, mod_ref, g_ref, w_ref, qg_ref, kg_ref, cos_ref, sin_ref,
                q_ref, k_ref, v_ref):
    x = x_ref[0]
    h = _rms(x, g_ref[0:1, :]) * (1.0 + mod_ref[0, 1:2, :]) + mod_ref[0, 0:1, :]
    y = jnp.dot(h.astype(BF16), w_ref[...], preferred_element_type=F32)
    cos = cos_ref[0]
    sin = sin_ref[0]
    scale = HEAD_DIM ** -0.5
    for c in range(D_MODEL // LANES):
        r = _head_norm_rope(y[:, c * LANES:(c + 1) * LANES], qg_ref[...], cos, sin) * scale
        q_ref[0, 2 * c] = r[:, :HEAD_DIM].astype(BF16)
        q_ref[0, 2 * c + 1] = r[:, HEAD_DIM:].astype(BF16)
    for c in range(KV_DIM // LANES):
        lo = D_MODEL + c * LANES
        r = _head_norm_rope(y[:, lo:lo + LANES], kg_ref[...], cos, sin)
        k_ref[0, 2 * c] = r[:, :HEAD_DIM].astype(BF16)
        k_ref[0, 2 * c + 1] = r[:, HEAD_DIM:].astype(BF16)
    for hd in range(N_KV_HEADS):
        lo = D_MODEL + KV_DIM + hd * HEAD_DIM
        v_ref[0, hd] = y[:, lo:lo + HEAD_DIM].astype(BF16)


def _qkv(xs, mod_i, g_i, wqkv, qg128, kg128, cos_t, sin_t):
    tm = 512
    return pl.pallas_call(
        _qkv_kernel,
        grid=(N_STACK, SEQ // tm),
        in_specs=[
            pl.BlockSpec((1, tm, D_MODEL), lambda b, j: (b, j, 0)),
            pl.BlockSpec((1, N_MOD, D_MODEL), lambda b, j: (b, 0, 0)),
            _small((4, D_MODEL)),
            _resident((D_MODEL, QKV_DIM)),
            _small((1, LANES)),
            _small((1, LANES)),
            pl.BlockSpec((1, tm, LANES), lambda b, j: (b // BATCH, j, 0)),
            pl.BlockSpec((1, tm, LANES), lambda b, j: (b // BATCH, j, 0)),
        ],
        out_specs=[
            pl.BlockSpec((1, N_HEADS, tm, HEAD_DIM), lambda b, j: (b, 0, j, 0)),
            pl.BlockSpec((1, N_KV_HEADS, tm, HEAD_DIM), lambda b, j: (b, 0, j, 0)),
            pl.BlockSpec((1, N_KV_HEADS, tm, HEAD_DIM), lambda b, j: (b, 0, j, 0)),
        ],
        out_shape=[
            jax.ShapeDtypeStruct((N_STACK, N_HEADS, SEQ, HEAD_DIM), BF16),
            jax.ShapeDtypeStruct((N_STACK, N_KV_HEADS, SEQ, HEAD_DIM), BF16),
            jax.ShapeDtypeStruct((N_STACK, N_KV_HEADS, SEQ, HEAD_DIM), BF16),
        ],
        compiler_params=_params(("arbitrary", "arbitrary")),
        name="attn_qkv",
    )(xs, mod_i, g_i, wqkv, qg128, kg128, cos_t, sin_t)


def _attn_kernel(*refs, tq, n_src):
    q_ref = refs[0]
    kv_refs = refs[1:1 + 2 * n_src]
    o_ref = refs[-1]
    q = q_ref[0].reshape(KV_GROUP * tq, HEAD_DIM)
    nt = (((1,), (1,)), ((), ()))
    scores = [lax.dot_general(q, kv_refs[2 * i][0, 0], nt, preferred_element_type=F32)
              for i in range(n_src)]
    m = scores[0].max(axis=-1, keepdims=True)
    for s in scores[1:]:
        m = jnp.maximum(m, s.max(axis=-1, keepdims=True))
    l = None
    o = None
    for i, s in enumerate(scores):
        p = jnp.exp(s - m)
        li = p.sum(axis=-1, keepdims=True)
        oi = jnp.dot(p.astype(BF16), kv_refs[2 * i + 1][0, 0], preferred_element_type=F32)
        l = li if l is None else l + li
        o = oi if o is None else o + oi
    o = (o / l).reshape(KV_GROUP, tq, HEAD_DIM)
    o_ref[0] = jnp.concatenate([o[i] for i in range(KV_GROUP)], axis=-1).astype(BF16)


def _attn_latent(q, k, v):
    tq = 256
    kern = functools.partial(_attn_kernel, tq=tq, n_src=2)
    kv_lat = pl.BlockSpec((1, 1, SEQ, HEAD_DIM), lambda b, g, i: (b, g, 0, 0))
    kv_ctx = pl.BlockSpec((1, 1, CTX_LEN, HEAD_DIM), lambda b, g, i: (CTX_BLK, g, b, 0))
    return pl.pallas_call(
        kern,
        grid=(BATCH, N_KV_HEADS, SEQ // tq),
        in_specs=[
            pl.BlockSpec((1, KV_GROUP, tq, HEAD_DIM), lambda b, g, i: (b, g, i, 0)),
            kv_lat, kv_lat, kv_ctx, kv_ctx,
        ],
        out_specs=pl.BlockSpec((1, tq, KV_GROUP * HEAD_DIM), lambda b, g, i: (b, i, g)),
        out_shape=jax.ShapeDtypeStruct((N_STACK, SEQ, D_MODEL), BF16),
        compiler_params=_params(("arbitrary", "arbitrary", "arbitrary")),
        name="attn_latent",
    )(q, k, v, k, v)


def _attn_ctx_kernel(q_ref, k_ref, v_ref, o_prev_ref, o_ref):
    del o_prev_ref
    _attn_kernel(q_ref, k_ref, v_ref, o_ref, tq=CTX_LEN, n_src=1)


def _attn_ctx(q, k, v, o):
    kv_ctx = pl.BlockSpec((1, 1, CTX_LEN, HEAD_DIM), lambda b, g: (CTX_BLK, g, b, 0))
    return pl.pallas_call(
        _attn_ctx_kernel,
        grid=(BATCH, N_KV_HEADS),
        in_specs=[
            pl.BlockSpec((1, KV_GROUP, CTX_LEN, HEAD_DIM), lambda b, g: (CTX_BLK, g, b, 0)),
            kv_ctx, kv_ctx,
            pl.BlockSpec(memory_space=pl.ANY),
        ],
        out_specs=pl.BlockSpec((1, CTX_LEN, KV_GROUP * HEAD_DIM), lambda b, g: (CTX_BLK, b, g)),
        out_shape=jax.ShapeDtypeStruct((N_STACK, SEQ, D_MODEL), BF16),
        input_output_aliases={3: 0},
        compiler_params=_params(("arbitrary", "arbitrary")),
        name="attn_ctx",
    )(q, k, v, o)


def _oproj_kernel(o_ref, x_ref, mod_ref, g_ref, w_ref, out_ref):
    y = jnp.dot(o_ref[0], w_ref[...], preferred_element_type=F32)
    out_ref[0] = x_ref[0] + mod_ref[0, 2:3, :] * _rms(y, g_ref[1:2, :])


def _oproj(o, xs, mod_i, g_i, wo, n_blk):
    tm = 512
    return pl.pallas_call(
        _oproj_kernel,
        grid=(n_blk, SEQ // tm),
        in_specs=[
            pl.BlockSpec((1, tm, D_MODEL), lambda b, j: (b, j, 0)),
            pl.BlockSpec((1, tm, D_MODEL), lambda b, j: (b, j, 0)),
            pl.BlockSpec((1, N_MOD, D_MODEL), lambda b, j: (b, 0, 0)),
            _small((4, D_MODEL)),
            _resident((D_MODEL, D_MODEL)),
        ],
        out_specs=pl.BlockSpec((1, tm, D_MODEL), lambda b, j: (b, j, 0)),
        out_shape=jax.ShapeDtypeStruct((N_STACK, SEQ, D_MODEL), F32),
        input_output_aliases={1: 0},
        compiler_params=_params(("arbitrary", "arbitrary")),
        name="attn_oproj",
    )(o, xs, mod_i, g_i, wo)


def _rope_tables():
    t = np.arange(SEQ)
    row = (t // GRID_W).astype(np.float64)
    col = (t % GRID_W).astype(np.float64)
    freqs = ROPE_THETA ** (-np.arange(ROPE_PAIRS_PER_AXIS, dtype=np.float64) / ROPE_PAIRS_PER_AXIS)
    ang = np.concatenate([row[:, None] * freqs[None, :], col[:, None] * freqs[None, :]], axis=-1)
    pair = (np.arange(LANES) % HEAD_DIM) // 2
    sign = np.where(np.arange(LANES) % 2 == 0, -1.0, 1.0)
    cos = np.cos(ang)[:, pair]
    sin = np.sin(ang)[:, pair] * sign[None, :]
    cos_t = np.stack([cos, np.ones_like(cos)]).astype(np.float32)
    sin_t = np.stack([sin, np.zeros_like(sin)]).astype(np.float32)
    return jnp.asarray(cos_t), jnp.asarray(sin_t)


def kernel(x, c, ctx, c_ctx, w_mod, b_mod, norm_g, conv_w_pw1, conv_b_pw1, conv_w_dw, conv_b_dw, conv_ln_g, conv_ln_b, conv_w_pw2, conv_b_pw2, attn_wq, attn_wk, attn_wv, attn_wo, attn_q_g, attn_k_g, ffn_w1, ffn_w3, ffn_w2):
    assert x.shape == (BATCH, SEQ, D_MODEL) and ctx.shape == (BATCH, CTX_LEN, D_MODEL)
    xs = jnp.concatenate([x, ctx.reshape(1, BATCH * CTX_LEN, D_MODEL)], axis=0)
    cc = jnp.concatenate([c, c_ctx[None, :],
                          jnp.zeros((MOD_ROWS - N_STACK, D_MODEL), F32)], axis=0)
    mod = _mod_table(cc, w_mod, b_mod)[:, :N_STACK].reshape(DEPTH, N_STACK, N_MOD, D_MODEL)
    cos_t, sin_t = _rope_tables()

    for i in range(DEPTH):
        need_ctx = i < DEPTH - 1
        n_blk = N_STACK if need_ctx else BATCH
        mod_i = mod[i]
        g_i = norm_g[i]
        j = i // 2
        if i % 2 == 0:
            u = _pw1(xs, mod_i, g_i, conv_w_pw1[j].astype(BF16), conv_b_pw1[j][None, :], n_blk)
            cp = (conv_w_dw[j], conv_b_dw[j][None, :], conv_ln_g[j][None, :],
                  conv_ln_b[j][None, :], conv_w_pw2[j].astype(BF16), conv_b_pw2[j][None, :])
            xs = _conv2(u, xs, mod_i, g_i, *cp, ctx=False)
            if need_ctx:
                xs = _conv2(u, xs, mod_i, g_i, *cp, ctx=True)
        else:
            wqkv = jnp.concatenate([attn_wq[j], attn_wk[j], attn_wv[j]], axis=1).astype(BF16)
            qg128 = jnp.tile(attn_q_g[j], LANES // HEAD_DIM)[None, :]
            kg128 = jnp.tile(attn_k_g[j], LANES // HEAD_DIM)[None, :]
            q, k, v = _qkv(xs, mod_i, g_i, wqkv, qg128, kg128, cos_t, sin_t)
            o = _attn_latent(q, k, v)
            if need_ctx:
                o = _attn_ctx(q, k, v, o)
            xs = _oproj(o, xs, mod_i, g_i, attn_wo[j].astype(BF16), n_blk)
        xs = _ffn(xs, mod_i, g_i, ffn_w1[i].astype(BF16), ffn_w3[i].astype(BF16),
                  ffn_w2[i].astype(BF16), n_blk, in_place=need_ctx)
    return xs
```

```python
import functools

import numpy as np
import jax
import jax.numpy as jnp
from jax import lax
from jax.experimental import pallas as pl
from jax.experimental.pallas import tpu as pltpu

D_MODEL = 1024
BATCH = 8
SEQ = 2048
DEPTH = 4
CTX_LEN = 256
GRID_W = 64
N_HEADS = 16
N_KV_HEADS = 4
HEAD_DIM = 64
KV_GROUP = N_HEADS // N_KV_HEADS
ROPE_PAIRS_PER_AXIS = HEAD_DIM // 4
ROPE_THETA = 10000.0
CONV_WIDTH = 31
CONV_PAD = CONV_WIDTH // 2
D_FF = 2816
N_MOD = 6
EPS = 1e-6
LOG2_E = 1.4426950408889634

N_STACK = BATCH + 1
CTX_BLK = BATCH
KV_DIM = N_KV_HEADS * HEAD_DIM
QKV_DIM = D_MODEL + 2 * KV_DIM
LANES = 128
HALO = 16
MOD_ROWS = 16
VMEM_LIMIT = 56 * 1024 * 1024

BF16 = jnp.bfloat16
F32 = jnp.float32


def _params(sem):
    return pltpu.CompilerParams(dimension_semantics=sem, vmem_limit_bytes=VMEM_LIMIT)


def _rms(x, g):
    return x * lax.rsqrt(jnp.mean(x * x, axis=-1, keepdims=True) + EPS) * g


def _sigmoid(x):
    return 1.0 / (1.0 + jnp.exp(-x))


def _resident(shape):
    nd = len(shape)
    return pl.BlockSpec(shape, lambda *_: (0,) * nd, pipeline_mode=pl.Buffered(1))


def _small(shape):
    nd = len(shape)
    return pl.BlockSpec(shape, lambda *_: (0,) * nd)


def _mod_kernel(cc_ref, w_ref, b_ref, o_ref):
    s = cc_ref[...]
    s = s * _sigmoid(s)
    o_ref[0] = jnp.dot(s.astype(BF16), w_ref[0].astype(BF16),
                       preferred_element_type=F32) + b_ref[0]


def _mod_table(cc, w_mod, b_mod):
    tn = 1536
    n_out = N_MOD * D_MODEL
    return pl.pallas_call(
        _mod_kernel,
        grid=(DEPTH, n_out // tn),
        in_specs=[
            pl.BlockSpec((MOD_ROWS, D_MODEL), lambda i, j: (0, 0)),
            pl.BlockSpec((1, D_MODEL, tn), lambda i, j: (i, 0, j)),
            pl.BlockSpec((1, 1, tn), lambda i, j: (i, 0, j)),
        ],
        out_specs=pl.BlockSpec((1, MOD_ROWS, tn), lambda i, j: (i, 0, j)),
        out_shape=jax.ShapeDtypeStruct((DEPTH, MOD_ROWS, n_out), F32),
        compiler_params=_params(("arbitrary", "arbitrary")),
        name="mod_table",
    )(cc, w_mod, b_mod.reshape(DEPTH, 1, n_out))


def _ffn_kernel(x_ref, mod_ref, g_ref, w1_ref, w3_ref, w2_ref, o_ref, *, n_chunks):
    x = x_ref[0]
    h = _rms(x, g_ref[2:3, :]) * (1.0 + mod_ref[0, 4:5, :]) + mod_ref[0, 3:4, :]
    hb = h.astype(BF16)
    fc = D_FF // n_chunks
    acc = None
    for c in range(n_chunks):
        a = jnp.dot(hb, w1_ref[:, c * fc:(c + 1) * fc], preferred_element_type=F32)
        b = jnp.dot(hb, w3_ref[:, c * fc:(c + 1) * fc], preferred_element_type=F32)
        act = (a * _sigmoid(a) * b).astype(BF16)
        part = jnp.dot(act, w2_ref[c * fc:(c + 1) * fc, :], preferred_element_type=F32)
        acc = part if acc is None else acc + part
    o_ref[0] = x + mod_ref[0, 5:6, :] * _rms(acc, g_ref[3:4, :])


def _ffn(xs, mod_i, g_i, w1, w3, w2, n_blk, in_place):
    tm = 512
    kern = functools.partial(_ffn_kernel, n_chunks=2)
    out_rows = N_STACK if in_place else n_blk
    return pl.pallas_call(
        kern,
        grid=(n_blk, SEQ // tm),
        in_specs=[
            pl.BlockSpec((1, tm, D_MODEL), lambda b, j: (b, j, 0)),
            pl.BlockSpec((1, N_MOD, D_MODEL), lambda b, j: (b, 0, 0)),
            _small((4, D_MODEL)),
            _resident((D_MODEL, D_FF)),
            _resident((D_MODEL, D_FF)),
            _resident((D_FF, D_MODEL)),
        ],
        out_specs=pl.BlockSpec((1, tm, D_MODEL), lambda b, j: (b, j, 0)),
        out_shape=jax.ShapeDtypeStruct((out_rows, SEQ, D_MODEL), F32),
        input_output_aliases={0: 0} if in_place else {},
        compiler_params=_params(("arbitrary", "arbitrary")),
        name="ffn",
    )(xs, mod_i, g_i, w1, w3, w2)


def _pw1_kernel(x_ref, mod_ref, g_ref, w_ref, b_ref, u_ref):
    x = x_ref[0]
    h = _rms(x, g_ref[0:1, :]) * (1.0 + mod_ref[0, 1:2, :]) + mod_ref[0, 0:1, :]
    hb = h.astype(BF16)
    a = jnp.dot(hb, w_ref[:, :D_MODEL], preferred_element_type=F32) + b_ref[:, :D_MODEL]
    gt = jnp.dot(hb, w_ref[:, D_MODEL:], preferred_element_type=F32) + b_ref[:, D_MODEL:]
    u_ref[0] = a * _sigmoid(gt)


def _pw1(xs, mod_i, g_i, w, b, n_blk):
    tm = 512
    return pl.pallas_call(
        _pw1_kernel,
        grid=(n_blk, SEQ // tm),
        in_specs=[
            pl.BlockSpec((1, tm, D_MODEL), lambda b_, j: (b_, j, 0)),
            pl.BlockSpec((1, N_MOD, D_MODEL), lambda b_, j: (b_, 0, 0)),
            _small((4, D_MODEL)),
            _resident((D_MODEL, 2 * D_MODEL)),
            _small((1, 2 * D_MODEL)),
        ],
        out_specs=pl.BlockSpec((1, tm, D_MODEL), lambda b_, j: (b_, j, 0)),
        out_shape=jax.ShapeDtypeStruct((N_STACK, SEQ, D_MODEL), F32),
        compiler_params=_params(("arbitrary", "arbitrary")),
        name="conv_pw1",
    )(xs, mod_i, g_i, w, b)


CONV_ROWS = 128


def _conv2_kernel(u_ref, up_ref, un_ref, x_ref, mod_ref, g_ref, wdw_ref, bdw_ref,
                  lng_ref, lnb_ref, w2_ref, b2_ref, o_ref, buf_ref, cv_ref, *, tm):
    j = pl.program_id(1)
    nj = pl.num_programs(1)
    base = HALO - CONV_PAD
    for c in range(D_MODEL // LANES):
        lanes = slice(c * LANES, (c + 1) * LANES)
        buf_ref[c, 0:HALO, :] = jnp.where(j > 0, up_ref[0, :, lanes], 0.0)
        buf_ref[c, HALO:HALO + tm, :] = u_ref[0, :, lanes]
        buf_ref[c, HALO + tm:HALO + tm + HALO, :] = jnp.where(j < nj - 1, un_ref[0, :, lanes], 0.0)

        def body(r, carry, c=c, lanes=lanes):
            r0 = pl.multiple_of(r * CONV_ROWS, CONV_ROWS)
            acc = jnp.zeros((CONV_ROWS, LANES), F32)
            for k in range(CONV_WIDTH):
                acc = acc + buf_ref[c, pl.ds(r0 + base + k, CONV_ROWS), :] * wdw_ref[k:k + 1, lanes]
            cv_ref[pl.ds(r0, CONV_ROWS), lanes] = acc + bdw_ref[:, lanes]
            return carry

        lax.fori_loop(0, tm // CONV_ROWS, body, 0)

    v = cv_ref[...]
    mu = jnp.mean(v, axis=-1, keepdims=True)
    vc = v - mu
    var = jnp.mean(vc * vc, axis=-1, keepdims=True)
    y = vc * lax.rsqrt(var + EPS) * lng_ref[...] + lnb_ref[...]
    y = y * _sigmoid(y)
    z = jnp.dot(y.astype(BF16), w2_ref[...], preferred_element_type=F32) + b2_ref[...]
    o_ref[0] = x_ref[0] + mod_ref[0, 2:3, :] * _rms(z, g_ref[1:2, :])


def _conv2(u, xs, mod_i, g_i, wdw, bdw, lng, lnb, w2, b2, *, ctx):
    if ctx:
        tm = CTX_LEN
        grid = (BATCH, 1)
        tile = lambda b, j: (CTX_BLK, b, 0)
        prev = lambda b, j: (CTX_BLK, 0, 0)
        nxt = lambda b, j: (CTX_BLK, 0, 0)
        mod_map = lambda b, j: (CTX_BLK, 0, 0)
    else:
        tm = 512
        grid = (BATCH, SEQ // tm)
        hb = tm // HALO
        tile = lambda b, j: (b, j, 0)
        prev = lambda b, j: (b, jnp.maximum(j * hb - 1, 0), 0)
        nxt = lambda b, j: (b, jnp.minimum((j + 1) * hb, SEQ // HALO - 1), 0)
        mod_map = lambda b, j: (b, 0, 0)
    kern = functools.partial(_conv2_kernel, tm=tm)
    return pl.pallas_call(
        kern,
        grid=grid,
        in_specs=[
            pl.BlockSpec((1, tm, D_MODEL), tile),
            pl.BlockSpec((1, HALO, D_MODEL), prev),
            pl.BlockSpec((1, HALO, D_MODEL), nxt),
            pl.BlockSpec((1, tm, D_MODEL), tile),
            pl.BlockSpec((1, N_MOD, D_MODEL), mod_map),
            _small((4, D_MODEL)),
            _small((CONV_WIDTH, D_MODEL)),
            _small((1, D_MODEL)),
            _small((1, D_MODEL)),
            _small((1, D_MODEL)),
            _resident((D_MODEL, D_MODEL)),
            _small((1, D_MODEL)),
        ],
        out_specs=pl.BlockSpec((1, tm, D_MODEL), tile),
        out_shape=jax.ShapeDtypeStruct((N_STACK, SEQ, D_MODEL), F32),
        scratch_shapes=[
            pltpu.VMEM((D_MODEL // LANES, tm + 2 * HALO, LANES), F32),
            pltpu.VMEM((tm, D_MODEL), F32),
        ],
        input_output_aliases={3: 0},
        compiler_params=_params(("arbitrary", "arbitrary")),
        name="conv_dw_pw2_ctx" if ctx else "conv_dw_pw2",
    )(u, u, u, xs, mod_i, g_i, wdw, bdw, lng, lnb, w2, b2)


def _head_norm_rope(x, g128, cos, sin_signed):
    lane = lax.broadcasted_iota(jnp.int32, x.shape, 1)
    first = lane < HEAD_DIM
    x2 = x * x
    ss_a = jnp.sum(jnp.where(first, x2, 0.0), axis=-1, keepdims=True)
    ss_b = jnp.sum(jnp.where(first, 0.0, x2), axis=-1, keepdims=True)
    ms = jnp.where(first, ss_a, ss_b) * (1.0 / HEAD_DIM)
    xn = x * lax.rsqrt(ms + EPS) * g128
    partner = jnp.where(lane % 2 == 0,
                        pltpu.roll(xn, LANES - 1, axis=1),
                        pltpu.roll(xn, 1, axis=1))
    return xn * cos + partner * sin_signed


def _qkv_kernel(x_ref, mod_ref, g_ref, w_ref, qg_ref, kg_ref, cos_ref, sin_ref,
                q_ref, k_ref, v_ref):
    x = x_ref[0]
    h = _rms(x, g_ref[0:1, :]) * (1.0 + mod_ref[0, 1:2, :]) + mod_ref[0, 0:1, :]
    y = jnp.dot(h.astype(BF16), w_ref[...], preferred_element_type=F32)
    cos = cos_ref[0]
    sin = sin_ref[0]
    scale = LOG2_E * HEAD_DIM ** -0.5
    for c in range(D_MODEL // LANES):
        r = _head_norm_rope(y[:, c * LANES:(c + 1) * LANES], qg_ref[...], cos, sin) * scale
        q_ref[0, 2 * c] = r[:, :HEAD_DIM].astype(BF16)
        q_ref[0, 2 * c + 1] = r[:, HEAD_DIM:].astype(BF16)
    for c in range(KV_DIM // LANES):
        lo = D_MODEL + c * LANES
        r = _head_norm_rope(y[:, lo:lo + LANES], kg_ref[...], cos, sin)
        k_ref[0, 2 * c] = r[:, :HEAD_DIM].astype(BF16)
        k_ref[0, 2 * c + 1] = r[:, HEAD_DIM:].astype(BF16)
    lane = lax.broadcasted_iota(jnp.int32, (x.shape[0], LANES), 1)
    tail = jnp.where(lane == HEAD_DIM, 1.0, 0.0)
    for c in range(KV_DIM // LANES):
        lo = D_MODEL + KV_DIM + c * LANES
        pair = y[:, lo:lo + LANES]
        v_ref[0, 2 * c] = jnp.where(lane < HEAD_DIM, pair, tail).astype(BF16)
        v_ref[0, 2 * c + 1] = jnp.where(lane < HEAD_DIM, pltpu.roll(pair, HEAD_DIM, axis=1),
                                        tail).astype(BF16)


def _qkv(xs, mod_i, g_i, wqkv, qg128, kg128, cos_t, sin_t):
    tm = 512
    return pl.pallas_call(
        _qkv_kernel,
        grid=(N_STACK, SEQ // tm),
        in_specs=[
            pl.BlockSpec((1, tm, D_MODEL), lambda b, j: (b, j, 0)),
            pl.BlockSpec((1, N_MOD, D_MODEL), lambda b, j: (b, 0, 0)),
            _small((4, D_MODEL)),
            _resident((D_MODEL, QKV_DIM)),
            _small((1, LANES)),
            _small((1, LANES)),
            pl.BlockSpec((1, tm, LANES), lambda b, j: (b // BATCH, j, 0)),
            pl.BlockSpec((1, tm, LANES), lambda b, j: (b // BATCH, j, 0)),
        ],
        out_specs=[
            pl.BlockSpec((1, N_HEADS, tm, HEAD_DIM), lambda b, j: (b, 0, j, 0)),
            pl.BlockSpec((1, N_KV_HEADS, tm, HEAD_DIM), lambda b, j: (b, 0, j, 0)),
            pl.BlockSpec((1, N_KV_HEADS, tm, LANES), lambda b, j: (b, 0, j, 0)),
        ],
        out_shape=[
            jax.ShapeDtypeStruct((N_STACK, N_HEADS, SEQ, HEAD_DIM), BF16),
            jax.ShapeDtypeStruct((N_STACK, N_KV_HEADS, SEQ, HEAD_DIM), BF16),
            jax.ShapeDtypeStruct((N_STACK, N_KV_HEADS, SEQ, LANES), BF16),
        ],
        compiler_params=_params(("arbitrary", "arbitrary")),
        name="attn_qkv",
    )(xs, mod_i, g_i, wqkv, qg128, kg128, cos_t, sin_t)


def _attn_kernel(*refs, tq, n_src, tk):
    q_ref = refs[0]
    kv_refs = refs[1:1 + 2 * n_src]
    o_ref = refs[-1]
    q = q_ref[0].reshape(KV_GROUP * tq, HEAD_DIM)
    nt = (((1,), (1,)), ((), ()))
    m = None
    acc = None
    for i in range(n_src):
        k_ref, v_ref = kv_refs[2 * i], kv_refs[2 * i + 1]
        n_keys = k_ref.shape[2]
        step = min(tk, n_keys)
        for lo in range(0, n_keys, step):
            s = lax.dot_general(q, k_ref[0, 0, lo:lo + step, :], nt, preferred_element_type=F32)
            mj = s.max(axis=-1, keepdims=True)
            m_new = mj if m is None else jnp.maximum(m, mj)
            p = jnp.exp2(s - m_new).astype(BF16)
            pv = jnp.dot(p, v_ref[0, 0, lo:lo + step, :], preferred_element_type=F32)
            acc = pv if acc is None else jnp.exp2(m - m_new) * acc + pv
            m = m_new
    o = acc[:, :HEAD_DIM] / acc[:, HEAD_DIM:HEAD_DIM + 1]
    o = o.reshape(KV_GROUP, tq, HEAD_DIM)
    o_ref[0] = jnp.concatenate([o[i] for i in range(KV_GROUP)], axis=-1).astype(BF16)


def _attn_latent(q, k, v):
    tq = 256
    kern = functools.partial(_attn_kernel, tq=tq, n_src=2, tk=512)
    kv_lat = pl.BlockSpec((1, 1, SEQ, HEAD_DIM), lambda b, g, i: (b, g, 0, 0))
    kv_ctx = pl.BlockSpec((1, 1, CTX_LEN, HEAD_DIM), lambda b, g, i: (CTX_BLK, g, b, 0))
    v_lat = pl.BlockSpec((1, 1, SEQ, LANES), lambda b, g, i: (b, g, 0, 0))
    v_ctx = pl.BlockSpec((1, 1, CTX_LEN, LANES), lambda b, g, i: (CTX_BLK, g, b, 0))
    return pl.pallas_call(
        kern,
        grid=(BATCH, N_KV_HEADS, SEQ // tq),
        in_specs=[
            pl.BlockSpec((1, KV_GROUP, tq, HEAD_DIM), lambda b, g, i: (b, g, i, 0)),
            kv_lat, v_lat, kv_ctx, v_ctx,
        ],
        out_specs=pl.BlockSpec((1, tq, KV_GROUP * HEAD_DIM), lambda b, g, i: (b, i, g)),
        out_shape=jax.ShapeDtypeStruct((N_STACK, SEQ, D_MODEL), BF16),
        compiler_params=_params(("arbitrary", "arbitrary", "arbitrary")),
        name="attn_latent",
    )(q, k, v, k, v)


def _attn_ctx_kernel(q_ref, k_ref, v_ref, o_prev_ref, o_ref):
    del o_prev_ref
    _attn_kernel(q_ref, k_ref, v_ref, o_ref, tq=CTX_LEN, n_src=1, tk=CTX_LEN)


def _attn_ctx(q, k, v, o):
    kv_ctx = pl.BlockSpec((1, 1, CTX_LEN, HEAD_DIM), lambda b, g: (CTX_BLK, g, b, 0))
    v_ctx = pl.BlockSpec((1, 1, CTX_LEN, LANES), lambda b, g: (CTX_BLK, g, b, 0))
    return pl.pallas_call(
        _attn_ctx_kernel,
        grid=(BATCH, N_KV_HEADS),
        in_specs=[
            pl.BlockSpec((1, KV_GROUP, CTX_LEN, HEAD_DIM), lambda b, g: (CTX_BLK, g, b, 0)),
            kv_ctx, v_ctx,
            pl.BlockSpec(memory_space=pl.ANY),
        ],
        out_specs=pl.BlockSpec((1, CTX_LEN, KV_GROUP * HEAD_DIM), lambda b, g: (CTX_BLK, b, g)),
        out_shape=jax.ShapeDtypeStruct((N_STACK, SEQ, D_MODEL), BF16),
        input_output_aliases={3: 0},
        compiler_params=_params(("arbitrary", "arbitrary")),
        name="attn_ctx",
    )(q, k, v, o)


def _oproj_kernel(o_ref, x_ref, mod_ref, g_ref, w_ref, out_ref):
    y = jnp.dot(o_ref[0], w_ref[...], preferred_element_type=F32)
    out_ref[0] = x_ref[0] + mod_ref[0, 2:3, :] * _rms(y, g_ref[1:2, :])


def _oproj(o, xs, mod_i, g_i, wo, n_blk):
    tm = 512
    return pl.pallas_call(
        _oproj_kernel,
        grid=(n_blk, SEQ // tm),
        in_specs=[
            pl.BlockSpec((1, tm, D_MODEL), lambda b, j: (b, j, 0)),
            pl.BlockSpec((1, tm, D_MODEL), lambda b, j: (b, j, 0)),
            pl.BlockSpec((1, N_MOD, D_MODEL), lambda b, j: (b, 0, 0)),
            _small((4, D_MODEL)),
            _resident((D_MODEL, D_MODEL)),
        ],
        out_specs=pl.BlockSpec((1, tm, D_MODEL), lambda b, j: (b, j, 0)),
        out_shape=jax.ShapeDtypeStruct((N_STACK, SEQ, D_MODEL), F32),
        input_output_aliases={1: 0},
        compiler_params=_params(("arbitrary", "arbitrary")),
        name="attn_oproj",
    )(o, xs, mod_i, g_i, wo)


def _rope_tables():
    t = np.arange(SEQ)
    row = (t // GRID_W).astype(np.float64)
    col = (t % GRID_W).astype(np.float64)
    freqs = ROPE_THETA ** (-np.arange(ROPE_PAIRS_PER_AXIS, dtype=np.float64) / ROPE_PAIRS_PER_AXIS)
    ang = np.concatenate([row[:, None] * freqs[None, :], col[:, None] * freqs[None, :]], axis=-1)
    pair = (np.arange(LANES) % HEAD_DIM) // 2
    sign = np.where(np.arange(LANES) % 2 == 0, -1.0, 1.0)
    cos = np.cos(ang)[:, pair]
    sin = np.sin(ang)[:, pair] * sign[None, :]
    cos_t = np.stack([cos, np.ones_like(cos)]).astype(np.float32)
    sin_t = np.stack([sin, np.zeros_like(sin)]).astype(np.float32)
    return jnp.asarray(cos_t), jnp.asarray(sin_t)


def kernel(x, c, ctx, c_ctx, w_mod, b_mod, norm_g, conv_w_pw1, conv_b_pw1, conv_w_dw, conv_b_dw, conv_ln_g, conv_ln_b, conv_w_pw2, conv_b_pw2, attn_wq, attn_wk, attn_wv, attn_wo, attn_q_g, attn_k_g, ffn_w1, ffn_w3, ffn_w2):
    assert x.shape == (BATCH, SEQ, D_MODEL) and ctx.shape == (BATCH, CTX_LEN, D_MODEL)
    xs = jnp.concatenate([x, ctx.reshape(1, BATCH * CTX_LEN, D_MODEL)], axis=0)
    cc = jnp.concatenate([c, c_ctx[None, :],
                          jnp.zeros((MOD_ROWS - N_STACK, D_MODEL), F32)], axis=0)
    mod = _mod_table(cc, w_mod, b_mod)[:, :N_STACK].reshape(DEPTH, N_STACK, N_MOD, D_MODEL)
    cos_t, sin_t = _rope_tables()

    for i in range(DEPTH):
        need_ctx = i < DEPTH - 1
        n_blk = N_STACK if need_ctx else BATCH
        mod_i = mod[i]
        g_i = norm_g[i]
        j = i // 2
        if i % 2 == 0:
            u = _pw1(xs, mod_i, g_i, conv_w_pw1[j].astype(BF16), conv_b_pw1[j][None, :], n_blk)
            cp = (conv_w_dw[j], conv_b_dw[j][None, :], conv_ln_g[j][None, :],
                  conv_ln_b[j][None, :], conv_w_pw2[j].astype(BF16), conv_b_pw2[j][None, :])
            xs = _conv2(u, xs, mod_i, g_i, *cp, ctx=False)
            if need_ctx:
                xs = _conv2(u, xs, mod_i, g_i, *cp, ctx=True)
        else:
            wqkv = jnp.concatenate([attn_wq[j], attn_wk[j], attn_wv[j]], axis=1).astype(BF16)
            qg128 = jnp.tile(attn_q_g[j], LANES // HEAD_DIM)[None, :]
            kg128 = jnp.tile(attn_k_g[j], LANES // HEAD_DIM)[None, :]
            q, k, v = _qkv(xs, mod_i, g_i, wqkv, qg128, kg128, cos_t, sin_t)
            o = _attn_latent(q, k, v)
            if need_ctx:
                o = _attn_ctx(q, k, v, o)
            xs = _oproj(o, xs, mod_i, g_i, attn_wo[j].astype(BF16), n_blk)
        xs = _ffn(xs, mod_i, g_i, ffn_w1[i].astype(BF16), ffn_w3[i].astype(BF16),
                  ffn_w2[i].astype(BF16), n_blk, in_place=need_ctx)
    return xs
```

```python
import functools

import numpy as np
import jax
import jax.numpy as jnp
from jax import lax
from jax.experimental import pallas as pl
from jax.experimental.pallas import tpu as pltpu

D_MODEL = 1024
BATCH = 8
SEQ = 2048
DEPTH = 4
CTX_LEN = 256
GRID_W = 64
N_HEADS = 16
N_KV_HEADS = 4
HEAD_DIM = 64
KV_GROUP = N_HEADS // N_KV_HEADS
ROPE_PAIRS_PER_AXIS = HEAD_DIM // 4
ROPE_THETA = 10000.0
CONV_WIDTH = 31
CONV_PAD = CONV_WIDTH // 2
D_FF = 2816
N_MOD = 6
EPS = 1e-6
LOG2_E = 1.4426950408889634

N_STACK = BATCH + 1
CTX_BLK = BATCH
KV_DIM = N_KV_HEADS * HEAD_DIM
QKV_DIM = D_MODEL + 2 * KV_DIM
LANES = 128
HALO = 16
MOD_ROWS = 16
V_ROWS = HEAD_DIM + 16
VMEM_LIMIT = 56 * 1024 * 1024

BF16 = jnp.bfloat16
F32 = jnp.float32


def _params(sem):
    return pltpu.CompilerParams(dimension_semantics=sem, vmem_limit_bytes=VMEM_LIMIT)


def _rms(x, g):
    return x * lax.rsqrt(jnp.mean(x * x, axis=-1, keepdims=True) + EPS) * g


def _sigmoid(x):
    return 1.0 / (1.0 + jnp.exp(-x))


def _resident(shape, layer):
    nd = len(shape)
    return pl.BlockSpec((None,) + shape, lambda *_: (layer,) + (0,) * nd,
                        pipeline_mode=pl.Buffered(1))


def _small(shape):
    nd = len(shape)
    return pl.BlockSpec(shape, lambda *_: (0,) * nd)


def _mod_kernel(cc_ref, w_ref, b_ref, o_ref):
    s = cc_ref[...]
    s = s * _sigmoid(s)
    o_ref[0] = jnp.dot(s.astype(BF16), w_ref[0].astype(BF16),
                       preferred_element_type=F32) + b_ref[0]


def _mod_table(cc, w_mod, b_mod):
    tn = 1536
    n_out = N_MOD * D_MODEL
    return pl.pallas_call(
        _mod_kernel,
        grid=(DEPTH, n_out // tn),
        in_specs=[
            pl.BlockSpec((MOD_ROWS, D_MODEL), lambda i, j: (0, 0)),
            pl.BlockSpec((1, D_MODEL, tn), lambda i, j: (i, 0, j)),
            pl.BlockSpec((1, 1, tn), lambda i, j: (i, 0, j)),
        ],
        out_specs=pl.BlockSpec((1, MOD_ROWS, tn), lambda i, j: (i, 0, j)),
        out_shape=jax.ShapeDtypeStruct((DEPTH, MOD_ROWS, n_out), F32),
        compiler_params=_params(("arbitrary", "arbitrary")),
        name="mod_table",
    )(cc, w_mod, b_mod.reshape(DEPTH, 1, n_out))


def _ffn_kernel(x_ref, mod_ref, g_ref, w1_ref, w3_ref, w2_ref, o_ref, *, n_chunks):
    x = x_ref[0]
    h = _rms(x, g_ref[2:3, :]) * (1.0 + mod_ref[0, 4:5, :]) + mod_ref[0, 3:4, :]
    hb = h.astype(BF16)
    fc = D_FF // n_chunks
    acc = None
    for c in range(n_chunks):
        a = jnp.dot(hb, w1_ref[:, c * fc:(c + 1) * fc], preferred_element_type=F32)
        b = jnp.dot(hb, w3_ref[:, c * fc:(c + 1) * fc], preferred_element_type=F32)
        act = (a * _sigmoid(a) * b).astype(BF16)
        part = jnp.dot(act, w2_ref[c * fc:(c + 1) * fc, :], preferred_element_type=F32)
        acc = part if acc is None else acc + part
    o_ref[0] = x + mod_ref[0, 5:6, :] * _rms(acc, g_ref[3:4, :])


def _ffn(xs, mod_i, g_i, w1, w3, w2, layer, n_blk, in_place):
    tm = 512
    kern = functools.partial(_ffn_kernel, n_chunks=2)
    out_rows = N_STACK if in_place else n_blk
    return pl.pallas_call(
        kern,
        grid=(n_blk, SEQ // tm),
        in_specs=[
            pl.BlockSpec((1, tm, D_MODEL), lambda b, j: (b, j, 0)),
            pl.BlockSpec((1, N_MOD, D_MODEL), lambda b, j: (b, 0, 0)),
            _small((4, D_MODEL)),
            _resident((D_MODEL, D_FF), layer),
            _resident((D_MODEL, D_FF), layer),
            _resident((D_FF, D_MODEL), layer),
        ],
        out_specs=pl.BlockSpec((1, tm, D_MODEL), lambda b, j: (b, j, 0)),
        out_shape=jax.ShapeDtypeStruct((out_rows, SEQ, D_MODEL), F32),
        input_output_aliases={0: 0} if in_place else {},
        compiler_params=_params(("arbitrary", "arbitrary")),
        name="ffn",
    )(xs, mod_i, g_i, w1, w3, w2)


def _pw1_kernel(x_ref, mod_ref, g_ref, w_ref, b_ref, u_ref):
    x = x_ref[0]
    h = _rms(x, g_ref[0:1, :]) * (1.0 + mod_ref[0, 1:2, :]) + mod_ref[0, 0:1, :]
    hb = h.astype(BF16)
    a = jnp.dot(hb, w_ref[:, :D_MODEL], preferred_element_type=F32) + b_ref[:, :D_MODEL]
    gt = jnp.dot(hb, w_ref[:, D_MODEL:], preferred_element_type=F32) + b_ref[:, D_MODEL:]
    u_ref[0] = a * _sigmoid(gt)


def _pw1(xs, mod_i, g_i, w, b, layer, n_blk):
    tm = 512
    return pl.pallas_call(
        _pw1_kernel,
        grid=(n_blk, SEQ // tm),
        in_specs=[
            pl.BlockSpec((1, tm, D_MODEL), lambda b_, j: (b_, j, 0)),
            pl.BlockSpec((1, N_MOD, D_MODEL), lambda b_, j: (b_, 0, 0)),
            _small((4, D_MODEL)),
            _resident((D_MODEL, 2 * D_MODEL), layer),
            _small((1, 2 * D_MODEL)),
        ],
        out_specs=pl.BlockSpec((1, tm, D_MODEL), lambda b_, j: (b_, j, 0)),
        out_shape=jax.ShapeDtypeStruct((N_STACK, SEQ, D_MODEL), F32),
        compiler_params=_params(("arbitrary", "arbitrary")),
        name="conv_pw1",
    )(xs, mod_i, g_i, w, b)


CONV_ROWS = 128


def _conv2_kernel(u_ref, up_ref, un_ref, x_ref, mod_ref, g_ref, wdw_ref, bdw_ref,
                  lng_ref, lnb_ref, w2_ref, b2_ref, o_ref, buf_ref, cv_ref, *, tm):
    j = pl.program_id(1)
    nj = pl.num_programs(1)
    base = HALO - CONV_PAD
    for c in range(D_MODEL // LANES):
        lanes = slice(c * LANES, (c + 1) * LANES)
        buf_ref[c, 0:HALO, :] = jnp.where(j > 0, up_ref[0, :, lanes], 0.0)
        buf_ref[c, HALO:HALO + tm, :] = u_ref[0, :, lanes]
        buf_ref[c, HALO + tm:HALO + tm + HALO, :] = jnp.where(j < nj - 1, un_ref[0, :, lanes], 0.0)

        def body(r, carry, c=c, lanes=lanes):
            r0 = pl.multiple_of(r * CONV_ROWS, CONV_ROWS)
            acc = jnp.zeros((CONV_ROWS, LANES), F32)
            for k in range(CONV_WIDTH):
                acc = acc + buf_ref[c, pl.ds(r0 + base + k, CONV_ROWS), :] * wdw_ref[k:k + 1, lanes]
            cv_ref[pl.ds(r0, CONV_ROWS), lanes] = acc + bdw_ref[:, lanes]
            return carry

        lax.fori_loop(0, tm // CONV_ROWS, body, 0)

    v = cv_ref[...]
    mu = jnp.mean(v, axis=-1, keepdims=True)
    vc = v - mu
    var = jnp.mean(vc * vc, axis=-1, keepdims=True)
    y = vc * lax.rsqrt(var + EPS) * lng_ref[...] + lnb_ref[...]
    y = y * _sigmoid(y)
    z = jnp.dot(y.astype(BF16), w2_ref[...], preferred_element_type=F32) + b2_ref[...]
    o_ref[0] = x_ref[0] + mod_ref[0, 2:3, :] * _rms(z, g_ref[1:2, :])


def _conv2(u, xs, mod_i, g_i, wdw, bdw, lng, lnb, w2, b2, *, layer, ctx):
    if ctx:
        tm = CTX_LEN
        grid = (BATCH, 1)
        tile = lambda b, j: (CTX_BLK, b, 0)
        prev = lambda b, j: (CTX_BLK, 0, 0)
        nxt = lambda b, j: (CTX_BLK, 0, 0)
        mod_map = lambda b, j: (CTX_BLK, 0, 0)
    else:
        tm = 512
        grid = (BATCH, SEQ // tm)
        hb = tm // HALO
        tile = lambda b, j: (b, j, 0)
        prev = lambda b, j: (b, jnp.maximum(j * hb - 1, 0), 0)
        nxt = lambda b, j: (b, jnp.minimum((j + 1) * hb, SEQ // HALO - 1), 0)
        mod_map = lambda b, j: (b, 0, 0)
    kern = functools.partial(_conv2_kernel, tm=tm)
    return pl.pallas_call(
        kern,
        grid=grid,
        in_specs=[
            pl.BlockSpec((1, tm, D_MODEL), tile),
            pl.BlockSpec((1, HALO, D_MODEL), prev),
            pl.BlockSpec((1, HALO, D_MODEL), nxt),
            pl.BlockSpec((1, tm, D_MODEL), tile),
            pl.BlockSpec((1, N_MOD, D_MODEL), mod_map),
            _small((4, D_MODEL)),
            _small((CONV_WIDTH, D_MODEL)),
            _small((1, D_MODEL)),
            _small((1, D_MODEL)),
            _small((1, D_MODEL)),
            _resident((D_MODEL, D_MODEL), layer),
            _small((1, D_MODEL)),
        ],
        out_specs=pl.BlockSpec((1, tm, D_MODEL), tile),
        out_shape=jax.ShapeDtypeStruct((N_STACK, SEQ, D_MODEL), F32),
        scratch_shapes=[
            pltpu.VMEM((D_MODEL // LANES, tm + 2 * HALO, LANES), F32),
            pltpu.VMEM((tm, D_MODEL), F32),
        ],
        input_output_aliases={3: 0},
        compiler_params=_params(("arbitrary", "arbitrary")),
        name="conv_dw_pw2_ctx" if ctx else "conv_dw_pw2",
    )(u, u, u, xs, mod_i, g_i, wdw, bdw, lng, lnb, w2, b2)


def _head_norm_rope(x, g128, cos, sin_signed):
    lane = lax.broadcasted_iota(jnp.int32, x.shape, 1)
    first = lane < HEAD_DIM
    x2 = x * x
    ss_a = jnp.sum(jnp.where(first, x2, 0.0), axis=-1, keepdims=True)
    ss_b = jnp.sum(jnp.where(first, 0.0, x2), axis=-1, keepdims=True)
    ms = jnp.where(first, ss_a, ss_b) * (1.0 / HEAD_DIM)
    xn = x * lax.rsqrt(ms + EPS) * g128
    partner = jnp.where(lane % 2 == 0,
                        pltpu.roll(xn, LANES - 1, axis=1),
                        pltpu.roll(xn, 1, axis=1))
    return xn * cos + partner * sin_signed


def _qkv_kernel(x_ref, mod_ref, g_ref, w_ref, qg_ref, kg_ref, cos_ref, sin_ref,
                q_ref, k_ref, v_ref):
    x = x_ref[0]
    h = _rms(x, g_ref[0:1, :]) * (1.0 + mod_ref[0, 1:2, :]) + mod_ref[0, 0:1, :]
    y = jnp.dot(h.astype(BF16), w_ref[...], preferred_element_type=F32)
    cos = cos_ref[0]
    sin = sin_ref[0]
    scale = LOG2_E * HEAD_DIM ** -0.5
    for c in range(D_MODEL // LANES):
        r = _head_norm_rope(y[:, c * LANES:(c + 1) * LANES], qg_ref[...], cos, sin) * scale
        rt = r.T.astype(BF16)
        q_ref[0, 2 * c] = rt[:HEAD_DIM]
        q_ref[0, 2 * c + 1] = rt[HEAD_DIM:]
    for c in range(KV_DIM // LANES):
        lo = D_MODEL + c * LANES
        r = _head_norm_rope(y[:, lo:lo + LANES], kg_ref[...], cos, sin)
        k_ref[0, 2 * c] = r[:, :HEAD_DIM].astype(BF16)
        k_ref[0, 2 * c + 1] = r[:, HEAD_DIM:].astype(BF16)
    row = lax.broadcasted_iota(jnp.int32, (V_ROWS - HEAD_DIM, x.shape[0]), 0)
    tail = jnp.where(row == 0, 1.0, 0.0).astype(BF16)
    for c in range(KV_DIM // LANES):
        lo = D_MODEL + KV_DIM + c * LANES
        pt = y[:, lo:lo + LANES].T.astype(BF16)
        for half in range(2):
            v_ref[0, 2 * c + half, 0:HEAD_DIM, :] = pt[half * HEAD_DIM:(half + 1) * HEAD_DIM]
            v_ref[0, 2 * c + half, HEAD_DIM:V_ROWS, :] = tail


def _qkv(xs, mod_i, g_i, wqkv, qg128, kg128, cos_t, sin_t, layer):
    tm = 512
    return pl.pallas_call(
        _qkv_kernel,
        grid=(N_STACK, SEQ // tm),
        in_specs=[
            pl.BlockSpec((1, tm, D_MODEL), lambda b, j: (b, j, 0)),
            pl.BlockSpec((1, N_MOD, D_MODEL), lambda b, j: (b, 0, 0)),
            _small((4, D_MODEL)),
            _resident((D_MODEL, QKV_DIM), layer),
            _small((1, LANES)),
            _small((1, LANES)),
            pl.BlockSpec((1, tm, LANES), lambda b, j: (b // BATCH, j, 0)),
            pl.BlockSpec((1, tm, LANES), lambda b, j: (b // BATCH, j, 0)),
        ],
        out_specs=[
            pl.BlockSpec((1, N_HEADS, HEAD_DIM, tm), lambda b, j: (b, 0, 0, j)),
            pl.BlockSpec((1, N_KV_HEADS, tm, HEAD_DIM), lambda b, j: (b, 0, j, 0)),
            pl.BlockSpec((1, N_KV_HEADS, V_ROWS, tm), lambda b, j: (b, 0, 0, j)),
        ],
        out_shape=[
            jax.ShapeDtypeStruct((N_STACK, N_HEADS, HEAD_DIM, SEQ), BF16),
            jax.ShapeDtypeStruct((N_STACK, N_KV_HEADS, SEQ, HEAD_DIM), BF16),
            jax.ShapeDtypeStruct((N_STACK, N_KV_HEADS, V_ROWS, SEQ), BF16),
        ],
        compiler_params=_params(("arbitrary", "arbitrary")),
        name="attn_qkv",
    )(xs, mod_i, g_i, wqkv, qg128, kg128, cos_t, sin_t)


def _attn_kernel(*refs, n_src, tk, lag):
    q_ref = refs[0]
    kv_refs = refs[1:1 + 2 * n_src]
    o_ref = refs[-1]
    chunks = []
    for i in range(n_src):
        k_ref, vt_ref = kv_refs[2 * i], kv_refs[2 * i + 1]
        n_keys = k_ref.shape[2]
        step = min(tk, n_keys)
        chunks += [(k_ref, vt_ref, lo, step) for lo in range(0, n_keys, step)]
    items = [(c, h) for c in chunks for h in range(KV_GROUP)]
    m = [None] * KV_GROUP
    acc = [None] * KV_GROUP
    scores = {}
    for idx in range(len(items) + lag):
        if idx < len(items):
            (k_ref, _, lo, step), h = items[idx]
            scores[idx] = jnp.dot(k_ref[0, 0, lo:lo + step, :], q_ref[0, h],
                                  preferred_element_type=F32)
        if idx >= lag:
            (_, vt_ref, lo, step), h = items[idx - lag]
            st = scores.pop(idx - lag)
            mj = st.max(axis=0, keepdims=True)
            m_new = mj if m[h] is None else jnp.maximum(m[h], mj)
            pt = jnp.exp2(st - m_new).astype(BF16)
            pv = jnp.dot(vt_ref[0, 0, :, lo:lo + step], pt, preferred_element_type=F32)
            acc[h] = pv if acc[h] is None else jnp.exp2(m[h] - m_new) * acc[h] + pv
            m[h] = m_new
    outs = [a[:HEAD_DIM] / a[HEAD_DIM:HEAD_DIM + 1] for a in acc]
    o_ref[0] = jnp.concatenate(outs, axis=0).T.astype(BF16)


def _attn_latent(q, k, vt):
    tq = 256
    kern = functools.partial(_attn_kernel, n_src=2, tk=256, lag=6)
    k_lat = pl.BlockSpec((1, 1, SEQ, HEAD_DIM), lambda b, g, i: (b, g, 0, 0))
    k_ctx = pl.BlockSpec((1, 1, CTX_LEN, HEAD_DIM), lambda b, g, i: (CTX_BLK, g, b, 0))
    vt_lat = pl.BlockSpec((1, 1, V_ROWS, SEQ), lambda b, g, i: (b, g, 0, 0))
    vt_ctx = pl.BlockSpec((1, 1, V_ROWS, CTX_LEN), lambda b, g, i: (CTX_BLK, g, 0, b))
    return pl.pallas_call(
        kern,
        grid=(BATCH, N_KV_HEADS, SEQ // tq),
        in_specs=[
            pl.BlockSpec((1, KV_GROUP, HEAD_DIM, tq), lambda b, g, i: (b, g, 0, i)),
            k_lat, vt_lat, k_ctx, vt_ctx,
        ],
        out_specs=pl.BlockSpec((1, tq, KV_GROUP * HEAD_DIM), lambda b, g, i: (b, i, g)),
        out_shape=jax.ShapeDtypeStruct((N_STACK, SEQ, D_MODEL), BF16),
        compiler_params=_params(("arbitrary", "arbitrary", "arbitrary")),
        name="attn_latent",
    )(q, k, vt, k, vt)


def _attn_ctx_kernel(q_ref, k_ref, vt_ref, o_prev_ref, o_ref):
    del o_prev_ref
    _attn_kernel(q_ref, k_ref, vt_ref, o_ref, n_src=1, tk=CTX_LEN, lag=2)


def _attn_ctx(q, k, vt, o):
    return pl.pallas_call(
        _attn_ctx_kernel,
        grid=(BATCH, N_KV_HEADS),
        in_specs=[
            pl.BlockSpec((1, KV_GROUP, HEAD_DIM, CTX_LEN), lambda b, g: (CTX_BLK, g, 0, b)),
            pl.BlockSpec((1, 1, CTX_LEN, HEAD_DIM), lambda b, g: (CTX_BLK, g, b, 0)),
            pl.BlockSpec((1, 1, V_ROWS, CTX_LEN), lambda b, g: (CTX_BLK, g, 0, b)),
            pl.BlockSpec(memory_space=pl.ANY),
        ],
        out_specs=pl.BlockSpec((1, CTX_LEN, KV_GROUP * HEAD_DIM), lambda b, g: (CTX_BLK, b, g)),
        out_shape=jax.ShapeDtypeStruct((N_STACK, SEQ, D_MODEL), BF16),
        input_output_aliases={3: 0},
        compiler_params=_params(("arbitrary", "arbitrary")),
        name="attn_ctx",
    )(q, k, vt, o)


def _oproj_kernel(o_ref, x_ref, mod_ref, g_ref, w_ref, out_ref):
    y = jnp.dot(o_ref[0], w_ref[...], preferred_element_type=F32)
    out_ref[0] = x_ref[0] + mod_ref[0, 2:3, :] * _rms(y, g_ref[1:2, :])


def _oproj(o, xs, mod_i, g_i, wo, layer, n_blk):
    tm = 512
    return pl.pallas_call(
        _oproj_kernel,
        grid=(n_blk, SEQ // tm),
        in_specs=[
            pl.BlockSpec((1, tm, D_MODEL), lambda b, j: (b, j, 0)),
            pl.BlockSpec((1, tm, D_MODEL), lambda b, j: (b, j, 0)),
            pl.BlockSpec((1, N_MOD, D_MODEL), lambda b, j: (b, 0, 0)),
            _small((4, D_MODEL)),
            _resident((D_MODEL, D_MODEL), layer),
        ],
        out_specs=pl.BlockSpec((1, tm, D_MODEL), lambda b, j: (b, j, 0)),
        out_shape=jax.ShapeDtypeStruct((N_STACK, SEQ, D_MODEL), F32),
        input_output_aliases={1: 0},
        compiler_params=_params(("arbitrary", "arbitrary")),
        name="attn_oproj",
    )(o, xs, mod_i, g_i, wo)


def _rope_tables():
    t = np.arange(SEQ)
    row = (t // GRID_W).astype(np.float64)
    col = (t % GRID_W).astype(np.float64)
    freqs = ROPE_THETA ** (-np.arange(ROPE_PAIRS_PER_AXIS, dtype=np.float64) / ROPE_PAIRS_PER_AXIS)
    ang = np.concatenate([row[:, None] * freqs[None, :], col[:, None] * freqs[None, :]], axis=-1)
    pair = (np.arange(LANES) % HEAD_DIM) // 2
    sign = np.where(np.arange(LANES) % 2 == 0, -1.0, 1.0)
    cos = np.cos(ang)[:, pair]
    sin = np.sin(ang)[:, pair] * sign[None, :]
    cos_t = np.stack([cos, np.ones_like(cos)]).astype(np.float32)
    sin_t = np.stack([sin, np.zeros_like(sin)]).astype(np.float32)
    return jnp.asarray(cos_t), jnp.asarray(sin_t)


def kernel(x, c, ctx, c_ctx, w_mod, b_mod, norm_g, conv_w_pw1, conv_b_pw1, conv_w_dw, conv_b_dw, conv_ln_g, conv_ln_b, conv_w_pw2, conv_b_pw2, attn_wq, attn_wk, attn_wv, attn_wo, attn_q_g, attn_k_g, ffn_w1, ffn_w3, ffn_w2):
    assert x.shape == (BATCH, SEQ, D_MODEL) and ctx.shape == (BATCH, CTX_LEN, D_MODEL)
    xs = jnp.concatenate([x, ctx.reshape(1, BATCH * CTX_LEN, D_MODEL)], axis=0)
    cc = jnp.concatenate([c, c_ctx[None, :],
                          jnp.zeros((MOD_ROWS - N_STACK, D_MODEL), F32)], axis=0)
    mod = _mod_table(cc, w_mod, b_mod)[:, :N_STACK].reshape(DEPTH, N_STACK, N_MOD, D_MODEL)
    cos_t, sin_t = _rope_tables()

    w_pw1 = conv_w_pw1.astype(BF16)
    w_pw2 = conv_w_pw2.astype(BF16)
    wqkv = jnp.concatenate([attn_wq, attn_wk, attn_wv], axis=2).astype(BF16)
    wo = attn_wo.astype(BF16)
    w1 = ffn_w1.astype(BF16)
    w3 = ffn_w3.astype(BF16)
    w2 = ffn_w2.astype(BF16)

    for i in range(DEPTH):
        need_ctx = i < DEPTH - 1
        n_blk = N_STACK if need_ctx else BATCH
        mod_i = mod[i]
        g_i = norm_g[i]
        j = i // 2
        if i % 2 == 0:
            u = _pw1(xs, mod_i, g_i, w_pw1, conv_b_pw1[j][None, :], j, n_blk)
            cp = (conv_w_dw[j], conv_b_dw[j][None, :], conv_ln_g[j][None, :],
                  conv_ln_b[j][None, :], w_pw2, conv_b_pw2[j][None, :])
            xs = _conv2(u, xs, mod_i, g_i, *cp, layer=j, ctx=False)
            if need_ctx:
                xs = _conv2(u, xs, mod_i, g_i, *cp, layer=j, ctx=True)
        else:
            qg128 = jnp.tile(attn_q_g[j], LANES // HEAD_DIM)[None, :]
            kg128 = jnp.tile(attn_k_g[j], LANES // HEAD_DIM)[None, :]
            q, k, v = _qkv(xs, mod_i, g_i, wqkv, qg128, kg128, cos_t, sin_t, j)
            o = _attn_latent(q, k, v)
            if need_ctx:
                o = _attn_ctx(q, k, v, o)
            xs = _oproj(o, xs, mod_i, g_i, wo, j, n_blk)
        xs = _ffn(xs, mod_i, g_i, w1, w3, w2, i, n_blk, in_place=need_ctx)
    return xs
```

```python
import functools

import numpy as np
import jax
import jax.numpy as jnp
from jax import lax
from jax.experimental import pallas as pl
from jax.experimental.pallas import tpu as pltpu

D_MODEL = 1024
BATCH = 8
SEQ = 2048
DEPTH = 4
CTX_LEN = 256
GRID_W = 64
N_HEADS = 16
N_KV_HEADS = 4
HEAD_DIM = 64
KV_GROUP = N_HEADS // N_KV_HEADS
ROPE_PAIRS_PER_AXIS = HEAD_DIM // 4
ROPE_THETA = 10000.0
CONV_WIDTH = 31
CONV_PAD = CONV_WIDTH // 2
D_FF = 2816
N_MOD = 6
EPS = 1e-6
LOG2_E = 1.4426950408889634

N_STACK = BATCH + 1
CTX_BLK = BATCH
KV_DIM = N_KV_HEADS * HEAD_DIM
QKV_DIM = D_MODEL + 2 * KV_DIM
LANES = 128
HALO = 16
MOD_ROWS = 16
V_ROWS = HEAD_DIM + 16
VMEM_LIMIT = 56 * 1024 * 1024

BF16 = jnp.bfloat16
F32 = jnp.float32


def _params(sem):
    return pltpu.CompilerParams(dimension_semantics=sem, vmem_limit_bytes=VMEM_LIMIT)


def _rms(x, g):
    return x * lax.rsqrt(jnp.mean(x * x, axis=-1, keepdims=True) + EPS) * g


def _sigmoid(x):
    return 1.0 / (1.0 + jnp.exp(-x))


def _resident(shape, layer):
    nd = len(shape)
    return pl.BlockSpec((None,) + shape, lambda *_: (layer,) + (0,) * nd,
                        pipeline_mode=pl.Buffered(1))


def _small(shape):
    nd = len(shape)
    return pl.BlockSpec(shape, lambda *_: (0,) * nd)


def _mod_kernel(cc_ref, w_ref, b_ref, o_ref):
    s = cc_ref[...]
    s = s * _sigmoid(s)
    o_ref[0] = jnp.dot(s.astype(BF16), w_ref[0].astype(BF16),
                       preferred_element_type=F32) + b_ref[0]


def _mod_table(cc, w_mod, b_mod):
    tn = 1536
    n_out = N_MOD * D_MODEL
    return pl.pallas_call(
        _mod_kernel,
        grid=(DEPTH, n_out // tn),
        in_specs=[
            pl.BlockSpec((MOD_ROWS, D_MODEL), lambda i, j: (0, 0)),
            pl.BlockSpec((1, D_MODEL, tn), lambda i, j: (i, 0, j)),
            pl.BlockSpec((1, 1, tn), lambda i, j: (i, 0, j)),
        ],
        out_specs=pl.BlockSpec((1, MOD_ROWS, tn), lambda i, j: (i, 0, j)),
        out_shape=jax.ShapeDtypeStruct((DEPTH, MOD_ROWS, n_out), F32),
        compiler_params=_params(("arbitrary", "arbitrary")),
        name="mod_table",
    )(cc, w_mod, b_mod.reshape(DEPTH, 1, n_out))


def _ffn_kernel(x_ref, mod_ref, g_ref, w1_ref, w3_ref, w2_ref, o_ref, *, n_chunks, n_sub):
    rows = x_ref.shape[1] // n_sub
    fc = D_FF // n_chunks
    xs = [x_ref[0, s * rows:(s + 1) * rows, :] for s in range(n_sub)]
    hb = [(_rms(x, g_ref[2:3, :]) * (1.0 + mod_ref[0, 4:5, :]) + mod_ref[0, 3:4, :]).astype(BF16)
          for x in xs]
    acc = [None] * n_sub
    for c in range(n_chunks):
        cols = slice(c * fc, (c + 1) * fc)
        ab = [(jnp.dot(h, w1_ref[:, cols], preferred_element_type=F32),
               jnp.dot(h, w3_ref[:, cols], preferred_element_type=F32)) for h in hb]
        for s, (a, b) in enumerate(ab):
            act = (a * _sigmoid(a) * b).astype(BF16)
            part = jnp.dot(act, w2_ref[cols, :], preferred_element_type=F32)
            acc[s] = part if acc[s] is None else acc[s] + part
    for s in range(n_sub):
        o_ref[0, s * rows:(s + 1) * rows, :] = (
            xs[s] + mod_ref[0, 5:6, :] * _rms(acc[s], g_ref[3:4, :]))


def _ffn(xs, mod_i, g_i, w1, w3, w2, layer, n_blk, in_place):
    tm = 1024
    kern = functools.partial(_ffn_kernel, n_chunks=2, n_sub=4)
    out_rows = N_STACK if in_place else n_blk
    return pl.pallas_call(
        kern,
        grid=(n_blk, SEQ // tm),
        in_specs=[
            pl.BlockSpec((1, tm, D_MODEL), lambda b, j: (b, j, 0)),
            pl.BlockSpec((1, N_MOD, D_MODEL), lambda b, j: (b, 0, 0)),
            _small((4, D_MODEL)),
            _resident((D_MODEL, D_FF), layer),
            _resident((D_MODEL, D_FF), layer),
            _resident((D_FF, D_MODEL), layer),
        ],
        out_specs=pl.BlockSpec((1, tm, D_MODEL), lambda b, j: (b, j, 0)),
        out_shape=jax.ShapeDtypeStruct((out_rows, SEQ, D_MODEL), F32),
        input_output_aliases={0: 0} if in_place else {},
        compiler_params=_params(("arbitrary", "arbitrary")),
        name="ffn",
    )(xs, mod_i, g_i, w1, w3, w2)


def _pw1_kernel(x_ref, mod_ref, g_ref, w_ref, b_ref, u_ref):
    x = x_ref[0]
    h = _rms(x, g_ref[0:1, :]) * (1.0 + mod_ref[0, 1:2, :]) + mod_ref[0, 0:1, :]
    hb = h.astype(BF16)
    a = jnp.dot(hb, w_ref[:, :D_MODEL], preferred_element_type=F32) + b_ref[:, :D_MODEL]
    gt = jnp.dot(hb, w_ref[:, D_MODEL:], preferred_element_type=F32) + b_ref[:, D_MODEL:]
    u_ref[0] = a * _sigmoid(gt)


def _pw1(xs, mod_i, g_i, w, b, layer, n_blk):
    tm = 512
    return pl.pallas_call(
        _pw1_kernel,
        grid=(n_blk, SEQ // tm),
        in_specs=[
            pl.BlockSpec((1, tm, D_MODEL), lambda b_, j: (b_, j, 0)),
            pl.BlockSpec((1, N_MOD, D_MODEL), lambda b_, j: (b_, 0, 0)),
            _small((4, D_MODEL)),
            _resident((D_MODEL, 2 * D_MODEL), layer),
            _small((1, 2 * D_MODEL)),
        ],
        out_specs=pl.BlockSpec((1, tm, D_MODEL), lambda b_, j: (b_, j, 0)),
        out_shape=jax.ShapeDtypeStruct((N_STACK, SEQ, D_MODEL), F32),
        compiler_params=_params(("arbitrary", "arbitrary")),
        name="conv_pw1",
    )(xs, mod_i, g_i, w, b)


CONV_ROWS = 128


def _conv2_kernel(u_ref, up_ref, un_ref, x_ref, mod_ref, g_ref, wdw_ref, bdw_ref,
                  lng_ref, lnb_ref, w2_ref, b2_ref, o_ref, buf_ref, cv_ref, *, tm):
    j = pl.program_id(1)
    nj = pl.num_programs(1)
    base = HALO - CONV_PAD
    for c in range(D_MODEL // LANES):
        lanes = slice(c * LANES, (c + 1) * LANES)
        buf_ref[c, 0:HALO, :] = jnp.where(j > 0, up_ref[0, :, lanes], 0.0)
        buf_ref[c, HALO:HALO + tm, :] = u_ref[0, :, lanes]
        buf_ref[c, HALO + tm:HALO + tm + HALO, :] = jnp.where(j < nj - 1, un_ref[0, :, lanes], 0.0)

        def body(r, carry, c=c, lanes=lanes):
            r0 = pl.multiple_of(r * CONV_ROWS, CONV_ROWS)
            acc = jnp.zeros((CONV_ROWS, LANES), F32)
            for k in range(CONV_WIDTH):
                acc = acc + buf_ref[c, pl.ds(r0 + base + k, CONV_ROWS), :] * wdw_ref[k:k + 1, lanes]
            cv_ref[pl.ds(r0, CONV_ROWS), lanes] = acc + bdw_ref[:, lanes]
            return carry

        lax.fori_loop(0, tm // CONV_ROWS, body, 0)

    v = cv_ref[...]
    mu = jnp.mean(v, axis=-1, keepdims=True)
    vc = v - mu
    var = jnp.mean(vc * vc, axis=-1, keepdims=True)
    y = vc * lax.rsqrt(var + EPS) * lng_ref[...] + lnb_ref[...]
    y = y * _sigmoid(y)
    z = jnp.dot(y.astype(BF16), w2_ref[...], preferred_element_type=F32) + b2_ref[...]
    o_ref[0] = x_ref[0] + mod_ref[0, 2:3, :] * _rms(z, g_ref[1:2, :])


def _conv2(u, xs, mod_i, g_i, wdw, bdw, lng, lnb, w2, b2, *, layer, ctx):
    if ctx:
        tm = CTX_LEN
        grid = (BATCH, 1)
        tile = lambda b, j: (CTX_BLK, b, 0)
        prev = lambda b, j: (CTX_BLK, 0, 0)
        nxt = lambda b, j: (CTX_BLK, 0, 0)
        mod_map = lambda b, j: (CTX_BLK, 0, 0)
    else:
        tm = 512
        grid = (BATCH, SEQ // tm)
        hb = tm // HALO
        tile = lambda b, j: (b, j, 0)
        prev = lambda b, j: (b, jnp.maximum(j * hb - 1, 0), 0)
        nxt = lambda b, j: (b, jnp.minimum((j + 1) * hb, SEQ // HALO - 1), 0)
        mod_map = lambda b, j: (b, 0, 0)
    kern = functools.partial(_conv2_kernel, tm=tm)
    return pl.pallas_call(
        kern,
        grid=grid,
        in_specs=[
            pl.BlockSpec((1, tm, D_MODEL), tile),
            pl.BlockSpec((1, HALO, D_MODEL), prev),
            pl.BlockSpec((1, HALO, D_MODEL), nxt),
            pl.BlockSpec((1, tm, D_MODEL), tile),
            pl.BlockSpec((1, N_MOD, D_MODEL), mod_map),
            _small((4, D_MODEL)),
            _small((CONV_WIDTH, D_MODEL)),
            _small((1, D_MODEL)),
            _small((1, D_MODEL)),
            _small((1, D_MODEL)),
            _resident((D_MODEL, D_MODEL), layer),
            _small((1, D_MODEL)),
        ],
        out_specs=pl.BlockSpec((1, tm, D_MODEL), tile),
        out_shape=jax.ShapeDtypeStruct((N_STACK, SEQ, D_MODEL), F32),
        scratch_shapes=[
            pltpu.VMEM((D_MODEL // LANES, tm + 2 * HALO, LANES), F32),
            pltpu.VMEM((tm, D_MODEL), F32),
        ],
        input_output_aliases={3: 0},
        compiler_params=_params(("arbitrary", "arbitrary")),
        name="conv_dw_pw2_ctx" if ctx else "conv_dw_pw2",
    )(u, u, u, xs, mod_i, g_i, wdw, bdw, lng, lnb, w2, b2)


HALF = HEAD_DIM // 2


def _head_norm_rope_t(xh, tab):
    xe, xo = xh[:HALF], xh[HALF:]
    ss = jnp.sum(xe * xe, axis=0, keepdims=True) + jnp.sum(xo * xo, axis=0, keepdims=True)
    rs = lax.rsqrt(ss * (1.0 / HEAD_DIM) + EPS)
    oe = (xe * tab[0:HALF] + xo * tab[HALF:2 * HALF]) * rs
    oo = (xe * tab[2 * HALF:3 * HALF] + xo * tab[3 * HALF:4 * HALF]) * rs
    return oe, oo


def _qkv_kernel(x_ref, mod_ref, g_ref, wt_ref, qtab_ref, ktab_ref, q_ref, k_ref, v_ref):
    x = x_ref[0]
    h = _rms(x, g_ref[0:1, :]) * (1.0 + mod_ref[0, 1:2, :]) + mod_ref[0, 0:1, :]
    nt = (((1,), (1,)), ((), ()))
    yt = lax.dot_general(wt_ref[...], h.astype(BF16), nt,
                         preferred_element_type=F32)
    qtab = qtab_ref[0]
    ktab = ktab_ref[0]
    for hd in range(N_HEADS):
        oe, oo = _head_norm_rope_t(yt[hd * HEAD_DIM:(hd + 1) * HEAD_DIM], qtab)
        q_ref[0, hd, 0:HALF, :] = oe.astype(BF16)
        q_ref[0, hd, HALF:HEAD_DIM, :] = oo.astype(BF16)
    for hd in range(N_KV_HEADS):
        lo = D_MODEL + hd * HEAD_DIM
        oe, oo = _head_norm_rope_t(yt[lo:lo + HEAD_DIM], ktab)
        k_ref[0, hd] = jnp.concatenate([oe, oo], axis=0).T.astype(BF16)
    row = lax.broadcasted_iota(jnp.int32, (V_ROWS - HEAD_DIM, x.shape[0]), 0)
    tail = jnp.where(row == 0, 1.0, 0.0).astype(BF16)
    for hd in range(N_KV_HEADS):
        lo = D_MODEL + KV_DIM + hd * HEAD_DIM
        v_ref[0, hd, 0:HEAD_DIM, :] = yt[lo:lo + HEAD_DIM].astype(BF16)
        v_ref[0, hd, HEAD_DIM:V_ROWS, :] = tail


def _qkv(xs, mod_i, g_i, wqkv_t, qtab, ktab, layer):
    tm = 512
    tab_spec = pl.BlockSpec((1, 4 * HALF, tm), lambda b, j: (b // BATCH, 0, j))
    return pl.pallas_call(
        _qkv_kernel,
        grid=(N_STACK, SEQ // tm),
        in_specs=[
            pl.BlockSpec((1, tm, D_MODEL), lambda b, j: (b, j, 0)),
            pl.BlockSpec((1, N_MOD, D_MODEL), lambda b, j: (b, 0, 0)),
            _small((4, D_MODEL)),
            _resident((QKV_DIM, D_MODEL), layer),
            tab_spec,
            tab_spec,
        ],
        out_specs=[
            pl.BlockSpec((1, N_HEADS, HEAD_DIM, tm), lambda b, j: (b, 0, 0, j)),
            pl.BlockSpec((1, N_KV_HEADS, tm, HEAD_DIM), lambda b, j: (b, 0, j, 0)),
            pl.BlockSpec((1, N_KV_HEADS, V_ROWS, tm), lambda b, j: (b, 0, 0, j)),
        ],
        out_shape=[
            jax.ShapeDtypeStruct((N_STACK, N_HEADS, HEAD_DIM, SEQ), BF16),
            jax.ShapeDtypeStruct((N_STACK, N_KV_HEADS, SEQ, HEAD_DIM), BF16),
            jax.ShapeDtypeStruct((N_STACK, N_KV_HEADS, V_ROWS, SEQ), BF16),
        ],
        compiler_params=_params(("arbitrary", "arbitrary")),
        name="attn_qkv",
    )(xs, mod_i, g_i, wqkv_t, qtab, ktab)


def _attn_kernel(*refs, n_src, tk, lag):
    q_ref = refs[0]
    kv_refs = refs[1:1 + 2 * n_src]
    o_ref = refs[-1]
    chunks = []
    for i in range(n_src):
        k_ref, vt_ref = kv_refs[2 * i], kv_refs[2 * i + 1]
        n_keys = k_ref.shape[2]
        step = min(tk, n_keys)
        chunks += [(k_ref, vt_ref, lo, step) for lo in range(0, n_keys, step)]
    items = [(c, h) for c in chunks for h in range(KV_GROUP)]
    m = [None] * KV_GROUP
    acc = [None] * KV_GROUP
    scores = {}
    for idx in range(len(items) + lag):
        if idx < len(items):
            (k_ref, _, lo, step), h = items[idx]
            scores[idx] = jnp.dot(k_ref[0, 0, lo:lo + step, :], q_ref[0, h],
                                  preferred_element_type=F32)
        if idx >= lag:
            (_, vt_ref, lo, step), h = items[idx - lag]
            st = scores.pop(idx - lag)
            mj = st.max(axis=0, keepdims=True)
            m_new = mj if m[h] is None else jnp.maximum(m[h], mj)
            pt = jnp.exp2(st - m_new).astype(BF16)
            pv = jnp.dot(vt_ref[0, 0, :, lo:lo + step], pt, preferred_element_type=F32)
            acc[h] = pv if acc[h] is None else jnp.exp2(m[h] - m_new) * acc[h] + pv
            m[h] = m_new
    outs = [a[:HEAD_DIM] / a[HEAD_DIM:HEAD_DIM + 1] for a in acc]
    o_ref[0] = jnp.concatenate(outs, axis=0).T.astype(BF16)


def _attn_latent(q, k, vt):
    tq = 256
    kern = functools.partial(_attn_kernel, n_src=2, tk=256, lag=6)
    k_lat = pl.BlockSpec((1, 1, SEQ, HEAD_DIM), lambda b, g, i: (b, g, 0, 0))
    k_ctx = pl.BlockSpec((1, 1, CTX_LEN, HEAD_DIM), lambda b, g, i: (CTX_BLK, g, b, 0))
    vt_lat = pl.BlockSpec((1, 1, V_ROWS, SEQ), lambda b, g, i: (b, g, 0, 0))
    vt_ctx = pl.BlockSpec((1, 1, V_ROWS, CTX_LEN), lambda b, g, i: (CTX_BLK, g, 0, b))
    return pl.pallas_call(
        kern,
        grid=(BATCH, N_KV_HEADS, SEQ // tq),
        in_specs=[
            pl.BlockSpec((1, KV_GROUP, HEAD_DIM, tq), lambda b, g, i: (b, g, 0, i)),
            k_lat, vt_lat, k_ctx, vt_ctx,
        ],
        out_specs=pl.BlockSpec((1, tq, KV_GROUP * HEAD_DIM), lambda b, g, i: (b, i, g)),
        out_shape=jax.ShapeDtypeStruct((N_STACK, SEQ, D_MODEL), BF16),
        compiler_params=_params(("arbitrary", "arbitrary", "arbitrary")),
        name="attn_latent",
    )(q, k, vt, k, vt)


def _attn_ctx_kernel(q_ref, k_ref, vt_ref, o_prev_ref, o_ref):
    del o_prev_ref
    _attn_kernel(q_ref, k_ref, vt_ref, o_ref, n_src=1, tk=CTX_LEN, lag=2)


def _attn_ctx(q, k, vt, o):
    return pl.pallas_call(
        _attn_ctx_kernel,
        grid=(BATCH, N_KV_HEADS),
        in_specs=[
            pl.BlockSpec((1, KV_GROUP, HEAD_DIM, CTX_LEN), lambda b, g: (CTX_BLK, g, 0, b)),
            pl.BlockSpec((1, 1, CTX_LEN, HEAD_DIM), lambda b, g: (CTX_BLK, g, b, 0)),
            pl.BlockSpec((1, 1, V_ROWS, CTX_LEN), lambda b, g: (CTX_BLK, g, 0, b)),
            pl.BlockSpec(memory_space=pl.ANY),
        ],
        out_specs=pl.BlockSpec((1, CTX_LEN, KV_GROUP * HEAD_DIM), lambda b, g: (CTX_BLK, b, g)),
        out_shape=jax.ShapeDtypeStruct((N_STACK, SEQ, D_MODEL), BF16),
        input_output_aliases={3: 0},
        compiler_params=_params(("arbitrary", "arbitrary")),
        name="attn_ctx",
    )(q, k, vt, o)


def _oproj_kernel(o_ref, x_ref, mod_ref, g_ref, w_ref, out_ref):
    y = jnp.dot(o_ref[0], w_ref[...], preferred_element_type=F32)
    out_ref[0] = x_ref[0] + mod_ref[0, 2:3, :] * _rms(y, g_ref[1:2, :])


def _oproj(o, xs, mod_i, g_i, wo, layer, n_blk):
    tm = 512
    return pl.pallas_call(
        _oproj_kernel,
        grid=(n_blk, SEQ // tm),
        in_specs=[
            pl.BlockSpec((1, tm, D_MODEL), lambda b, j: (b, j, 0)),
            pl.BlockSpec((1, tm, D_MODEL), lambda b, j: (b, j, 0)),
            pl.BlockSpec((1, N_MOD, D_MODEL), lambda b, j: (b, 0, 0)),
            _small((4, D_MODEL)),
            _resident((D_MODEL, D_MODEL), layer),
        ],
        out_specs=pl.BlockSpec((1, tm, D_MODEL), lambda b, j: (b, j, 0)),
        out_shape=jax.ShapeDtypeStruct((N_STACK, SEQ, D_MODEL), F32),
        input_output_aliases={1: 0},
        compiler_params=_params(("arbitrary", "arbitrary")),
        name="attn_oproj",
    )(o, xs, mod_i, g_i, wo)


def _rope_tables():
    t = np.arange(SEQ)
    row = (t // GRID_W).astype(np.float64)
    col = (t % GRID_W).astype(np.float64)
    freqs = ROPE_THETA ** (-np.arange(ROPE_PAIRS_PER_AXIS, dtype=np.float64) / ROPE_PAIRS_PER_AXIS)
    ang = np.concatenate([row[:, None] * freqs[None, :], col[:, None] * freqs[None, :]], axis=-1)
    cos = np.cos(ang).T
    sin = np.sin(ang).T
    cos_t = np.stack([cos, np.ones_like(cos)]).astype(np.float32)
    sin_t = np.stack([sin, np.zeros_like(sin)]).astype(np.float32)
    return jnp.asarray(cos_t), jnp.asarray(sin_t)


def _norm_rope_table(g, cos_t, sin_t, scale):
    ge = g[0::2][None, :, None] * scale
    go = g[1::2][None, :, None] * scale
    return jnp.concatenate([ge * cos_t, -(go * sin_t), ge * sin_t, go * cos_t], axis=1)


def _even_odd_columns(n_heads):
    d = np.arange(HEAD_DIM)
    within = np.concatenate([d[0::2], d[1::2]])
    return (np.arange(n_heads)[:, None] * HEAD_DIM + within[None, :]).reshape(-1)


def kernel(x, c, ctx, c_ctx, w_mod, b_mod, norm_g, conv_w_pw1, conv_b_pw1, conv_w_dw, conv_b_dw, conv_ln_g, conv_ln_b, conv_w_pw2, conv_b_pw2, attn_wq, attn_wk, attn_wv, attn_wo, attn_q_g, attn_k_g, ffn_w1, ffn_w3, ffn_w2):
    assert x.shape == (BATCH, SEQ, D_MODEL) and ctx.shape == (BATCH, CTX_LEN, D_MODEL)
    xs = jnp.concatenate([x, ctx.reshape(1, BATCH * CTX_LEN, D_MODEL)], axis=0)
    cc = jnp.concatenate([c, c_ctx[None, :],
                          jnp.zeros((MOD_ROWS - N_STACK, D_MODEL), F32)], axis=0)
    mod = _mod_table(cc, w_mod, b_mod)[:, :N_STACK].reshape(DEPTH, N_STACK, N_MOD, D_MODEL)
    cos_t, sin_t = _rope_tables()

    w_pw1 = conv_w_pw1.astype(BF16)
    w_pw2 = conv_w_pw2.astype(BF16)
    wqkv_t = jnp.concatenate([attn_wq[:, :, _even_odd_columns(N_HEADS)],
                              attn_wk[:, :, _even_odd_columns(N_KV_HEADS)],
                              attn_wv], axis=2).astype(BF16).transpose(0, 2, 1)
    wo = attn_wo.astype(BF16)
    w1 = ffn_w1.astype(BF16)
    w3 = ffn_w3.astype(BF16)
    w2 = ffn_w2.astype(BF16)

    for i in range(DEPTH):
        need_ctx = i < DEPTH - 1
        n_blk = N_STACK if need_ctx else BATCH
        mod_i = mod[i]
        g_i = norm_g[i]
        j = i // 2
        if i % 2 == 0:
            u = _pw1(xs, mod_i, g_i, w_pw1, conv_b_pw1[j][None, :], j, n_blk)
            cp = (conv_w_dw[j], conv_b_dw[j][None, :], conv_ln_g[j][None, :],
                  conv_ln_b[j][None, :], w_pw2, conv_b_pw2[j][None, :])
            xs = _conv2(u, xs, mod_i, g_i, *cp, layer=j, ctx=False)
            if need_ctx:
                xs = _conv2(u, xs, mod_i, g_i, *cp, layer=j, ctx=True)
        else:
            qtab = _norm_rope_table(attn_q_g[j], cos_t, sin_t, LOG2_E * HEAD_DIM ** -0.5)
            ktab = _norm_rope_table(attn_k_g[j], cos_t, sin_t, 1.0)
            q, k, v = _qkv(xs, mod_i, g_i, wqkv_t, qtab, ktab, j)
            o = _attn_latent(q, k, v)
            if need_ctx:
                o = _attn_ctx(q, k, v, o)
            xs = _oproj(o, xs, mod_i, g_i, wo, j, n_blk)
        xs = _ffn(xs, mod_i, g_i, w1, w3, w2, i, n_blk, in_place=need_ctx)
    return xs
```

```python
import functools

import numpy as np
import jax
import jax.numpy as jnp
from jax import lax
from jax.experimental import pallas as pl
from jax.experimental.pallas import tpu as pltpu

D_MODEL = 1024
BATCH = 8
SEQ = 2048
DEPTH = 4
CTX_LEN = 256
GRID_W = 64
N_HEADS = 16
N_KV_HEADS = 4
HEAD_DIM = 64
KV_GROUP = N_HEADS // N_KV_HEADS
ROPE_PAIRS_PER_AXIS = HEAD_DIM // 4
ROPE_THETA = 10000.0
CONV_WIDTH = 31
CONV_PAD = CONV_WIDTH // 2
D_FF = 2816
N_MOD = 6
EPS = 1e-6
LOG2_E = 1.4426950408889634

N_STACK = BATCH + 1
CTX_BLK = BATCH
KV_DIM = N_KV_HEADS * HEAD_DIM
QKV_DIM = D_MODEL + 2 * KV_DIM
LANES = 128
HALO = 16
MOD_ROWS = 16
V_ROWS = HEAD_DIM + 16
VMEM_LIMIT = 56 * 1024 * 1024

BF16 = jnp.bfloat16
F32 = jnp.float32


def _params(sem):
    return pltpu.CompilerParams(dimension_semantics=sem, vmem_limit_bytes=VMEM_LIMIT)


def _rms(x, g):
    return x * lax.rsqrt(jnp.mean(x * x, axis=-1, keepdims=True) + EPS) * g


def _sigmoid(x):
    return 1.0 / (1.0 + jnp.exp(-x))


def _resident(shape, layer):
    nd = len(shape)
    return pl.BlockSpec((None,) + shape, lambda *_: (layer,) + (0,) * nd,
                        pipeline_mode=pl.Buffered(1))


def _small(shape):
    nd = len(shape)
    return pl.BlockSpec(shape, lambda *_: (0,) * nd)


def _mod_kernel(cc_ref, w_ref, b_ref, o_ref):
    s = cc_ref[...]
    s = s * _sigmoid(s)
    o_ref[0] = jnp.dot(s.astype(BF16), w_ref[0].astype(BF16),
                       preferred_element_type=F32) + b_ref[0]


def _mod_table(cc, w_mod, b_mod):
    tn = 1536
    n_out = N_MOD * D_MODEL
    return pl.pallas_call(
        _mod_kernel,
        grid=(DEPTH, n_out // tn),
        in_specs=[
            pl.BlockSpec((MOD_ROWS, D_MODEL), lambda i, j: (0, 0)),
            pl.BlockSpec((1, D_MODEL, tn), lambda i, j: (i, 0, j)),
            pl.BlockSpec((1, 1, tn), lambda i, j: (i, 0, j)),
        ],
        out_specs=pl.BlockSpec((1, MOD_ROWS, tn), lambda i, j: (i, 0, j)),
        out_shape=jax.ShapeDtypeStruct((DEPTH, MOD_ROWS, n_out), F32),
        compiler_params=_params(("arbitrary", "arbitrary")),
        name="mod_table",
    )(cc, w_mod, b_mod.reshape(DEPTH, 1, n_out))


def _ffn_kernel(x_ref, mod_ref, g_ref, w1_ref, w3_ref, w2_ref, o_ref, *, n_chunks, n_sub):
    rows = x_ref.shape[1] // n_sub
    fc = D_FF // n_chunks
    xs = [x_ref[0, s * rows:(s + 1) * rows, :] for s in range(n_sub)]
    hb = [(_rms(x, g_ref[2:3, :]) * (1.0 + mod_ref[0, 4:5, :]) + mod_ref[0, 3:4, :]).astype(BF16)
          for x in xs]
    acc = [None] * n_sub
    for c in range(n_chunks):
        cols = slice(c * fc, (c + 1) * fc)
        ab = [(jnp.dot(h, w1_ref[:, cols], preferred_element_type=F32),
               jnp.dot(h, w3_ref[:, cols], preferred_element_type=F32)) for h in hb]
        for s, (a, b) in enumerate(ab):
            act = (a * _sigmoid(a) * b).astype(BF16)
            part = jnp.dot(act, w2_ref[cols, :], preferred_element_type=F32)
            acc[s] = part if acc[s] is None else acc[s] + part
    for s in range(n_sub):
        o_ref[0, s * rows:(s + 1) * rows, :] = (
            xs[s] + mod_ref[0, 5:6, :] * _rms(acc[s], g_ref[3:4, :]))


def _ffn(xs, mod_i, g_i, w1, w3, w2, layer, n_blk, in_place):
    tm = 1024
    kern = functools.partial(_ffn_kernel, n_chunks=2, n_sub=4)
    out_rows = N_STACK if in_place else n_blk
    return pl.pallas_call(
        kern,
        grid=(n_blk, SEQ // tm),
        in_specs=[
            pl.BlockSpec((1, tm, D_MODEL), lambda b, j: (b, j, 0)),
            pl.BlockSpec((1, N_MOD, D_MODEL), lambda b, j: (b, 0, 0)),
            _small((4, D_MODEL)),
            _resident((D_MODEL, D_FF), layer),
            _resident((D_MODEL, D_FF), layer),
            _resident((D_FF, D_MODEL), layer),
        ],
        out_specs=pl.BlockSpec((1, tm, D_MODEL), lambda b, j: (b, j, 0)),
        out_shape=jax.ShapeDtypeStruct((out_rows, SEQ, D_MODEL), F32),
        input_output_aliases={0: 0} if in_place else {},
        compiler_params=_params(("arbitrary", "arbitrary")),
        name="ffn",
    )(xs, mod_i, g_i, w1, w3, w2)


CONV_ROWS = 128


def _conv_kernel(x_ref, xp_ref, xn_ref, mod_ref, g_ref, w1_ref, b1_ref, wdw_ref, bdw_ref,
                 lng_ref, lnb_ref, w2_ref, b2_ref, o_ref, hb_ref, buf_ref, *, tm):
    j = pl.program_id(1)
    nj = pl.num_programs(1)
    win = tm + 2 * HALO
    n_grp = tm // CONV_ROWS

    def prenorm(x):
        h = _rms(x, g_ref[0:1, :]) * (1.0 + mod_ref[0, 1:2, :]) + mod_ref[0, 0:1, :]
        return h.astype(BF16)

    hb_ref[0:HALO, :] = prenorm(xp_ref[0])
    hb_ref[HALO:HALO + tm, :] = prenorm(x_ref[0])
    hb_ref[HALO + tm:win, :] = prenorm(xn_ref[0])

    def pointwise1(lo, hi):
        hb = hb_ref[lo:hi, :]
        a = jnp.dot(hb, w1_ref[:, :D_MODEL], preferred_element_type=F32) + b1_ref[:, :D_MODEL]
        gt = jnp.dot(hb, w1_ref[:, D_MODEL:], preferred_element_type=F32) + b1_ref[:, D_MODEL:]
        u = a * _sigmoid(gt)
        if lo < HALO or hi > HALO + tm:
            row = lax.broadcasted_iota(jnp.int32, (hi - lo, 1), 0) + lo
            valid = (((row >= HALO) | (j > 0)) & ((row < HALO + tm) | (j < nj - 1)))
            u = jnp.where(valid, u, 0.0)
        for c in range(D_MODEL // LANES):
            buf_ref[c, lo:hi, :] = u[:, c * LANES:(c + 1) * LANES]

    def conv_group(r):
        r0 = r * CONV_ROWS
        base = HALO - CONV_PAD
        cols = []
        for c in range(D_MODEL // LANES):
            lanes = slice(c * LANES, (c + 1) * LANES)
            acc = jnp.zeros((CONV_ROWS, LANES), F32)
            for k in range(CONV_WIDTH):
                lo = r0 + base + k
                acc = acc + buf_ref[c, lo:lo + CONV_ROWS, :] * wdw_ref[k:k + 1, lanes]
            cols.append(acc + bdw_ref[:, lanes])
        v = jnp.concatenate(cols, axis=1)
        mu = jnp.mean(v, axis=-1, keepdims=True)
        vc = v - mu
        var = jnp.mean(vc * vc, axis=-1, keepdims=True)
        y = vc * lax.rsqrt(var + EPS) * lng_ref[...] + lnb_ref[...]
        y = y * _sigmoid(y)
        z = jnp.dot(y.astype(BF16), w2_ref[...], preferred_element_type=F32) + b2_ref[...]
        rows = slice(r0, r0 + CONV_ROWS)
        o_ref[0, rows, :] = x_ref[0, rows, :] + mod_ref[0, 2:3, :] * _rms(z, g_ref[1:2, :])

    bounds = [0] + [(r + 1) * CONV_ROWS + 2 * HALO for r in range(n_grp)]
    pointwise1(bounds[0], bounds[1])
    for r in range(n_grp):
        if r + 1 < n_grp:
            pointwise1(bounds[r + 1], bounds[r + 2])
        conv_group(r)


def _conv_ctx_kernel(*refs, tm):
    _conv_kernel(*refs[:13], *refs[14:], tm=tm)


def _conv(src, dst, mod_i, g_i, w1, b1, wdw, bdw, lng, lnb, w2, b2, *, layer, ctx):
    if ctx:
        tm = CTX_LEN
        grid = (BATCH, 1)
        blk = src.shape[0] - 1
        tile = lambda b, j: (blk, b, 0)
        prev = lambda b, j: (blk, 0, 0)
        nxt = lambda b, j: (blk, 0, 0)
        mod_map = lambda b, j: (CTX_BLK, 0, 0)
        out_tile = lambda b, j: (CTX_BLK, b, 0)
    else:
        tm = 512
        grid = (BATCH, SEQ // tm)
        hb = tm // HALO
        tile = lambda b, j: (b, j, 0)
        prev = lambda b, j: (b, jnp.maximum(j * hb - 1, 0), 0)
        nxt = lambda b, j: (b, jnp.minimum((j + 1) * hb, SEQ // HALO - 1), 0)
        mod_map = lambda b, j: (b, 0, 0)
        out_tile = tile
    in_specs = [
        pl.BlockSpec((1, tm, D_MODEL), tile),
        pl.BlockSpec((1, HALO, D_MODEL), prev),
        pl.BlockSpec((1, HALO, D_MODEL), nxt),
        pl.BlockSpec((1, N_MOD, D_MODEL), mod_map),
        _small((4, D_MODEL)),
        _resident((D_MODEL, 2 * D_MODEL), layer),
        _small((1, 2 * D_MODEL)),
        _small((CONV_WIDTH, D_MODEL)),
        _small((1, D_MODEL)),
        _small((1, D_MODEL)),
        _small((1, D_MODEL)),
        _resident((D_MODEL, D_MODEL), layer),
        _small((1, D_MODEL)),
    ]
    args = [src, src, src, mod_i, g_i, w1, b1, wdw, bdw, lng, lnb, w2, b2]
    if ctx:
        kern = functools.partial(_conv_ctx_kernel, tm=tm)
        in_specs.append(pl.BlockSpec(memory_space=pl.ANY))
        args.append(dst)
        aliases = {len(args) - 1: 0}
    else:
        kern = functools.partial(_conv_kernel, tm=tm)
        aliases = {}
    return pl.pallas_call(
        kern,
        grid=grid,
        in_specs=in_specs,
        out_specs=pl.BlockSpec((1, tm, D_MODEL), out_tile),
        out_shape=jax.ShapeDtypeStruct((N_STACK, SEQ, D_MODEL), F32),
        scratch_shapes=[
            pltpu.VMEM((tm + 2 * HALO, D_MODEL), BF16),
            pltpu.VMEM((D_MODEL // LANES, tm + 2 * HALO, LANES), F32),
        ],
        input_output_aliases=aliases,
        compiler_params=_params(("arbitrary", "arbitrary")),
        name="conv_ctx" if ctx else "conv",
    )(*args)


HALF = HEAD_DIM // 2


def _head_norm_rope_t(xh, tab):
    xe, xo = xh[:HALF], xh[HALF:]
    ss = jnp.sum(xe * xe, axis=0, keepdims=True) + jnp.sum(xo * xo, axis=0, keepdims=True)
    rs = lax.rsqrt(ss * (1.0 / HEAD_DIM) + EPS)
    oe = (xe * tab[0:HALF] + xo * tab[HALF:2 * HALF]) * rs
    oo = (xe * tab[2 * HALF:3 * HALF] + xo * tab[3 * HALF:4 * HALF]) * rs
    return oe, oo


def _qkv_kernel(x_ref, mod_ref, g_ref, wt_ref, qtab_ref, ktab_ref, q_ref, k_ref, v_ref):
    x = x_ref[0]
    h = _rms(x, g_ref[0:1, :]) * (1.0 + mod_ref[0, 1:2, :]) + mod_ref[0, 0:1, :]
    nt = (((1,), (1,)), ((), ()))
    yt = lax.dot_general(wt_ref[...], h.astype(BF16), nt,
                         preferred_element_type=F32)
    qtab = qtab_ref[0]
    ktab = ktab_ref[0]
    for hd in range(N_HEADS):
        oe, oo = _head_norm_rope_t(yt[hd * HEAD_DIM:(hd + 1) * HEAD_DIM], qtab)
        q_ref[0, hd, 0:HALF, :] = oe.astype(BF16)
        q_ref[0, hd, HALF:HEAD_DIM, :] = oo.astype(BF16)
    for hd in range(N_KV_HEADS):
        lo = D_MODEL + hd * HEAD_DIM
        oe, oo = _head_norm_rope_t(yt[lo:lo + HEAD_DIM], ktab)
        k_ref[0, hd] = jnp.concatenate([oe, oo], axis=0).T.astype(BF16)
    row = lax.broadcasted_iota(jnp.int32, (V_ROWS - HEAD_DIM, x.shape[0]), 0)
    tail = jnp.where(row == 0, 1.0, 0.0).astype(BF16)
    for hd in range(N_KV_HEADS):
        lo = D_MODEL + KV_DIM + hd * HEAD_DIM
        v_ref[0, hd, 0:HEAD_DIM, :] = yt[lo:lo + HEAD_DIM].astype(BF16)
        v_ref[0, hd, HEAD_DIM:V_ROWS, :] = tail


def _qkv(xs, mod_i, g_i, wqkv_t, qtab, ktab, layer):
    tm = 512
    tab_spec = pl.BlockSpec((1, 4 * HALF, tm), lambda b, j: (b // BATCH, 0, j))
    return pl.pallas_call(
        _qkv_kernel,
        grid=(N_STACK, SEQ // tm),
        in_specs=[
            pl.BlockSpec((1, tm, D_MODEL), lambda b, j: (b, j, 0)),
            pl.BlockSpec((1, N_MOD, D_MODEL), lambda b, j: (b, 0, 0)),
            _small((4, D_MODEL)),
            _resident((QKV_DIM, D_MODEL), layer),
            tab_spec,
            tab_spec,
        ],
        out_specs=[
            pl.BlockSpec((1, N_HEADS, HEAD_DIM, tm), lambda b, j: (b, 0, 0, j)),
            pl.BlockSpec((1, N_KV_HEADS, tm, HEAD_DIM), lambda b, j: (b, 0, j, 0)),
            pl.BlockSpec((1, N_KV_HEADS, V_ROWS, tm), lambda b, j: (b, 0, 0, j)),
        ],
        out_shape=[
            jax.ShapeDtypeStruct((N_STACK, N_HEADS, HEAD_DIM, SEQ), BF16),
            jax.ShapeDtypeStruct((N_STACK, N_KV_HEADS, SEQ, HEAD_DIM), BF16),
            jax.ShapeDtypeStruct((N_STACK, N_KV_HEADS, V_ROWS, SEQ), BF16),
        ],
        compiler_params=_params(("arbitrary", "arbitrary")),
        name="attn_qkv",
    )(xs, mod_i, g_i, wqkv_t, qtab, ktab)


Q_COLS = 256


def _attn_kernel(*refs, n_src, tk, lag):
    q_ref = refs[0]
    kv_refs = refs[1:1 + 2 * n_src]
    o_ref = refs[-1]
    chunks = []
    for i in range(n_src):
        k_ref, vt_ref = kv_refs[2 * i], kv_refs[2 * i + 1]
        n_keys = k_ref.shape[2]
        step = min(tk, n_keys)
        chunks += [(k_ref, vt_ref, lo, step) for lo in range(0, n_keys, step)]
    tq = q_ref.shape[3]
    groups = [(h, qs) for qs in range(0, tq, Q_COLS) for h in range(KV_GROUP)]
    items = [(c, g) for c in chunks for g in range(len(groups))]
    m = [None] * len(groups)
    acc = [None] * len(groups)
    scores = {}
    for idx in range(len(items) + lag):
        if idx < len(items):
            (k_ref, _, lo, step), g = items[idx]
            h, qs = groups[g]
            scores[idx] = jnp.dot(k_ref[0, 0, lo:lo + step, :], q_ref[0, h, :, qs:qs + Q_COLS],
                                  preferred_element_type=F32)
        if idx >= lag:
            (_, vt_ref, lo, step), g = items[idx - lag]
            st = scores.pop(idx - lag)
            mj = st.max(axis=0, keepdims=True)
            m_new = mj if m[g] is None else jnp.maximum(m[g], mj)
            pt = jnp.exp2(st - m_new).astype(BF16)
            pv = jnp.dot(vt_ref[0, 0, :, lo:lo + step], pt, preferred_element_type=F32)
            acc[g] = pv if acc[g] is None else jnp.exp2(m[g] - m_new) * acc[g] + pv
            m[g] = m_new
    for qs in range(0, tq, Q_COLS):
        outs = [acc[g][:HEAD_DIM] / acc[g][HEAD_DIM:HEAD_DIM + 1]
                for g, (_, gq) in enumerate(groups) if gq == qs]
        o_ref[0, qs:qs + Q_COLS, :] = jnp.concatenate(outs, axis=0).T.astype(BF16)


def _attn_latent(q, k, vt):
    tq = 512
    kern = functools.partial(_attn_kernel, n_src=2, tk=256, lag=6)
    k_lat = pl.BlockSpec((1, 1, SEQ, HEAD_DIM), lambda b, g, i: (b, g, 0, 0))
    k_ctx = pl.BlockSpec((1, 1, CTX_LEN, HEAD_DIM), lambda b, g, i: (CTX_BLK, g, b, 0))
    vt_lat = pl.BlockSpec((1, 1, V_ROWS, SEQ), lambda b, g, i: (b, g, 0, 0))
    vt_ctx = pl.BlockSpec((1, 1, V_ROWS, CTX_LEN), lambda b, g, i: (CTX_BLK, g, 0, b))
    return pl.pallas_call(
        kern,
        grid=(BATCH, N_KV_HEADS, SEQ // tq),
        in_specs=[
            pl.BlockSpec((1, KV_GROUP, HEAD_DIM, tq), lambda b, g, i: (b, g, 0, i)),
            k_lat, vt_lat, k_ctx, vt_ctx,
        ],
        out_specs=pl.BlockSpec((1, tq, KV_GROUP * HEAD_DIM), lambda b, g, i: (b, i, g)),
        out_shape=jax.ShapeDtypeStruct((N_STACK, SEQ, D_MODEL), BF16),
        compiler_params=_params(("arbitrary", "arbitrary", "arbitrary")),
        name="attn_latent",
    )(q, k, vt, k, vt)


def _attn_ctx_kernel(q_ref, k_ref, vt_ref, o_prev_ref, o_ref):
    del o_prev_ref
    _attn_kernel(q_ref, k_ref, vt_ref, o_ref, n_src=1, tk=CTX_LEN, lag=2)


def _attn_ctx(q, k, vt, o):
    return pl.pallas_call(
        _attn_ctx_kernel,
        grid=(BATCH, N_KV_HEADS),
        in_specs=[
            pl.BlockSpec((1, KV_GROUP, HEAD_DIM, CTX_LEN), lambda b, g: (CTX_BLK, g, 0, b)),
            pl.BlockSpec((1, 1, CTX_LEN, HEAD_DIM), lambda b, g: (CTX_BLK, g, b, 0)),
            pl.BlockSpec((1, 1, V_ROWS, CTX_LEN), lambda b, g: (CTX_BLK, g, 0, b)),
            pl.BlockSpec(memory_space=pl.ANY),
        ],
        out_specs=pl.BlockSpec((1, CTX_LEN, KV_GROUP * HEAD_DIM), lambda b, g: (CTX_BLK, b, g)),
        out_shape=jax.ShapeDtypeStruct((N_STACK, SEQ, D_MODEL), BF16),
        input_output_aliases={3: 0},
        compiler_params=_params(("arbitrary", "arbitrary")),
        name="attn_ctx",
    )(q, k, vt, o)


def _oproj_kernel(o_ref, x_ref, mod_ref, g_ref, w_ref, out_ref):
    y = jnp.dot(o_ref[0], w_ref[...], preferred_element_type=F32)
    out_ref[0] = x_ref[0] + mod_ref[0, 2:3, :] * _rms(y, g_ref[1:2, :])


def _oproj(o, xs, mod_i, g_i, wo, layer, n_blk):
    tm = 512
    return pl.pallas_call(
        _oproj_kernel,
        grid=(n_blk, SEQ // tm),
        in_specs=[
            pl.BlockSpec((1, tm, D_MODEL), lambda b, j: (b, j, 0)),
            pl.BlockSpec((1, tm, D_MODEL), lambda b, j: (b, j, 0)),
            pl.BlockSpec((1, N_MOD, D_MODEL), lambda b, j: (b, 0, 0)),
            _small((4, D_MODEL)),
            _resident((D_MODEL, D_MODEL), layer),
        ],
        out_specs=pl.BlockSpec((1, tm, D_MODEL), lambda b, j: (b, j, 0)),
        out_shape=jax.ShapeDtypeStruct((N_STACK, SEQ, D_MODEL), F32),
        input_output_aliases={1: 0},
        compiler_params=_params(("arbitrary", "arbitrary")),
        name="attn_oproj",
    )(o, xs, mod_i, g_i, wo)


def _rope_tables():
    t = np.arange(SEQ)
    row = (t // GRID_W).astype(np.float64)
    col = (t % GRID_W).astype(np.float64)
    freqs = ROPE_THETA ** (-np.arange(ROPE_PAIRS_PER_AXIS, dtype=np.float64) / ROPE_PAIRS_PER_AXIS)
    ang = np.concatenate([row[:, None] * freqs[None, :], col[:, None] * freqs[None, :]], axis=-1)
    cos = np.cos(ang).T
    sin = np.sin(ang).T
    cos_t = np.stack([cos, np.ones_like(cos)]).astype(np.float32)
    sin_t = np.stack([sin, np.zeros_like(sin)]).astype(np.float32)
    return jnp.asarray(cos_t), jnp.asarray(sin_t)


def _norm_rope_table(g, cos_t, sin_t, scale):
    ge = g[0::2][None, :, None] * scale
    go = g[1::2][None, :, None] * scale
    return jnp.concatenate([ge * cos_t, -(go * sin_t), ge * sin_t, go * cos_t], axis=1)


def _even_odd_columns(n_heads):
    d = np.arange(HEAD_DIM)
    within = np.concatenate([d[0::2], d[1::2]])
    return (np.arange(n_heads)[:, None] * HEAD_DIM + within[None, :]).reshape(-1)


def kernel(x, c, ctx, c_ctx, w_mod, b_mod, norm_g, conv_w_pw1, conv_b_pw1, conv_w_dw, conv_b_dw, conv_ln_g, conv_ln_b, conv_w_pw2, conv_b_pw2, attn_wq, attn_wk, attn_wv, attn_wo, attn_q_g, attn_k_g, ffn_w1, ffn_w3, ffn_w2):
    assert x.shape == (BATCH, SEQ, D_MODEL) and ctx.shape == (BATCH, CTX_LEN, D_MODEL)
    xs = None
    cc = jnp.concatenate([c, c_ctx[None, :],
                          jnp.zeros((MOD_ROWS - N_STACK, D_MODEL), F32)], axis=0)
    mod = _mod_table(cc, w_mod, b_mod)[:, :N_STACK].reshape(DEPTH, N_STACK, N_MOD, D_MODEL)
    cos_t, sin_t = _rope_tables()

    w_pw1 = conv_w_pw1.astype(BF16)
    w_pw2 = conv_w_pw2.astype(BF16)
    wqkv_t = jnp.concatenate([attn_wq[:, :, _even_odd_columns(N_HEADS)],
                              attn_wk[:, :, _even_odd_columns(N_KV_HEADS)],
                              attn_wv], axis=2).astype(BF16).transpose(0, 2, 1)
    wo = attn_wo.astype(BF16)
    w1 = ffn_w1.astype(BF16)
    w3 = ffn_w3.astype(BF16)
    w2 = ffn_w2.astype(BF16)

    for i in range(DEPTH):
        need_ctx = i < DEPTH - 1
        n_blk = N_STACK if need_ctx else BATCH
        mod_i = mod[i]
        g_i = norm_g[i]
        j = i // 2
        if i % 2 == 0:
            assert need_ctx
            cp = (w_pw1, conv_b_pw1[j][None, :], conv_w_dw[j], conv_b_dw[j][None, :],
                  conv_ln_g[j][None, :], conv_ln_b[j][None, :], w_pw2, conv_b_pw2[j][None, :])
            src_x, src_c = (x, ctx.reshape(1, BATCH * CTX_LEN, D_MODEL)) if i == 0 else (xs, xs)
            new = _conv(src_x, None, mod_i, g_i, *cp, layer=j, ctx=False)
            xs = _conv(src_c, new, mod_i, g_i, *cp, layer=j, ctx=True)
        else:
            qtab = _norm_rope_table(attn_q_g[j], cos_t, sin_t, LOG2_E * HEAD_DIM ** -0.5)
            ktab = _norm_rope_table(attn_k_g[j], cos_t, sin_t, 1.0)
            q, k, v = _qkv(xs, mod_i, g_i, wqkv_t, qtab, ktab, j)
            o = _attn_latent(q, k, v)
            if need_ctx:
                o = _attn_ctx(q, k, v, o)
            xs = _oproj(o, xs, mod_i, g_i, wo, j, n_blk)
        xs = _ffn(xs, mod_i, g_i, w1, w3, w2, i, n_blk, in_place=need_ctx)
    return xs
```

```python
import functools

import numpy as np
import jax
import jax.numpy as jnp
from jax import lax
from jax.experimental import pallas as pl
from jax.experimental.pallas import tpu as pltpu

D_MODEL = 1024
BATCH = 8
SEQ = 2048
DEPTH = 4
CTX_LEN = 256
GRID_W = 64
N_HEADS = 16
N_KV_HEADS = 4
HEAD_DIM = 64
KV_GROUP = N_HEADS // N_KV_HEADS
ROPE_PAIRS_PER_AXIS = HEAD_DIM // 4
ROPE_THETA = 10000.0
CONV_WIDTH = 31
CONV_PAD = CONV_WIDTH // 2
D_FF = 2816
N_MOD = 6
EPS = 1e-6
LOG2_E = 1.4426950408889634

N_STACK = BATCH + 1
CTX_BLK = BATCH
KV_DIM = N_KV_HEADS * HEAD_DIM
QKV_DIM = D_MODEL + 2 * KV_DIM
LANES = 128
HALO = 16
MOD_ROWS = 16
V_ROWS = HEAD_DIM + 16
VMEM_LIMIT = 56 * 1024 * 1024

BF16 = jnp.bfloat16
F32 = jnp.float32


def _params(sem):
    return pltpu.CompilerParams(dimension_semantics=sem, vmem_limit_bytes=VMEM_LIMIT)


def _rms(x, g):
    return x * lax.rsqrt(jnp.mean(x * x, axis=-1, keepdims=True) + EPS) * g


def _sigmoid(x):
    return 1.0 / (1.0 + jnp.exp(-x))


def _resident(shape, layer):
    nd = len(shape)
    return pl.BlockSpec((None,) + shape, lambda *_: (layer,) + (0,) * nd,
                        pipeline_mode=pl.Buffered(1))


def _small(shape):
    nd = len(shape)
    return pl.BlockSpec(shape, lambda *_: (0,) * nd)


def _mod_kernel(cc_ref, w_ref, b_ref, o_ref):
    s = cc_ref[...]
    s = s * _sigmoid(s)
    o_ref[0] = jnp.dot(s.astype(BF16), w_ref[0].astype(BF16),
                       preferred_element_type=F32) + b_ref[0]


def _mod_table(cc, w_mod, b_mod):
    tn = 1536
    n_out = N_MOD * D_MODEL
    return pl.pallas_call(
        _mod_kernel,
        grid=(DEPTH, n_out // tn),
        in_specs=[
            pl.BlockSpec((MOD_ROWS, D_MODEL), lambda i, j: (0, 0)),
            pl.BlockSpec((1, D_MODEL, tn), lambda i, j: (i, 0, j)),
            pl.BlockSpec((1, 1, tn), lambda i, j: (i, 0, j)),
        ],
        out_specs=pl.BlockSpec((1, MOD_ROWS, tn), lambda i, j: (i, 0, j)),
        out_shape=jax.ShapeDtypeStruct((DEPTH, MOD_ROWS, n_out), F32),
        compiler_params=_params(("arbitrary", "arbitrary")),
        name="mod_table",
    )(cc, w_mod, b_mod.reshape(DEPTH, 1, n_out))


def _ffn_kernel(x_ref, mod_ref, g_ref, w1_ref, w3_ref, w2_ref, *rest, n_chunks, n_sub):
    *attn, o_ref = rest
    rows = x_ref.shape[1] // n_sub
    fc = D_FF // n_chunks
    xs = [x_ref[0, s * rows:(s + 1) * rows, :] for s in range(n_sub)]
    if attn:
        attn_ref, wo_ref = attn
        ys = [jnp.dot(attn_ref[0, s * rows:(s + 1) * rows, :], wo_ref[...],
                      preferred_element_type=F32) for s in range(n_sub)]
        xs = [x + mod_ref[0, 2:3, :] * _rms(y, g_ref[1:2, :]) for x, y in zip(xs, ys)]
    hb = [(_rms(x, g_ref[2:3, :]) * (1.0 + mod_ref[0, 4:5, :]) + mod_ref[0, 3:4, :]).astype(BF16)
          for x in xs]
    acc = [None] * n_sub
    for c in range(n_chunks):
        cols = slice(c * fc, (c + 1) * fc)
        ab = [(jnp.dot(h, w1_ref[:, cols], preferred_element_type=F32),
               jnp.dot(h, w3_ref[:, cols], preferred_element_type=F32)) for h in hb]
        for s, (a, b) in enumerate(ab):
            act = (a * _sigmoid(a) * b).astype(BF16)
            part = jnp.dot(act, w2_ref[cols, :], preferred_element_type=F32)
            acc[s] = part if acc[s] is None else acc[s] + part
    for s in range(n_sub):
        o_ref[0, s * rows:(s + 1) * rows, :] = (
            xs[s] + mod_ref[0, 5:6, :] * _rms(acc[s], g_ref[3:4, :]))


def _ffn(xs, mod_i, g_i, w1, w3, w2, layer, n_blk, in_place, attn=None):
    tm = 1024
    kern = functools.partial(_ffn_kernel, n_chunks=2, n_sub=4)
    out_rows = N_STACK if in_place else n_blk
    tile = pl.BlockSpec((1, tm, D_MODEL), lambda b, j: (b, j, 0))
    in_specs = [
        tile,
        pl.BlockSpec((1, N_MOD, D_MODEL), lambda b, j: (b, 0, 0)),
        _small((4, D_MODEL)),
        _resident((D_MODEL, D_FF), layer),
        _resident((D_MODEL, D_FF), layer),
        _resident((D_FF, D_MODEL), layer),
    ]
    args = [xs, mod_i, g_i, w1, w3, w2]
    if attn is not None:
        o, wo, attn_layer = attn
        in_specs += [tile, _resident((D_MODEL, D_MODEL), attn_layer)]
        args += [o, wo]
    return pl.pallas_call(
        kern,
        grid=(n_blk, SEQ // tm),
        in_specs=in_specs,
        out_specs=tile,
        out_shape=jax.ShapeDtypeStruct((out_rows, SEQ, D_MODEL), F32),
        input_output_aliases={0: 0} if in_place else {},
        compiler_params=_params(("arbitrary", "arbitrary")),
        name="ffn" if attn is None else "oproj_ffn",
    )(*args)


CONV_ROWS = 128


def _conv_kernel(x_ref, xp_ref, xn_ref, mod_ref, g_ref, w1_ref, b1_ref, wdw_ref, bdw_ref,
                 lng_ref, lnb_ref, w2_ref, b2_ref, o_ref, hb_ref, buf_ref, *, tm):
    j = pl.program_id(1)
    nj = pl.num_programs(1)
    win = tm + 2 * HALO
    n_grp = tm // CONV_ROWS

    def prenorm(x):
        h = _rms(x, g_ref[0:1, :]) * (1.0 + mod_ref[0, 1:2, :]) + mod_ref[0, 0:1, :]
        return h.astype(BF16)

    hb_ref[0:HALO, :] = prenorm(xp_ref[0])
    hb_ref[HALO:HALO + tm, :] = prenorm(x_ref[0])
    hb_ref[HALO + tm:win, :] = prenorm(xn_ref[0])

    def pointwise1(lo, hi):
        hb = hb_ref[lo:hi, :]
        a = jnp.dot(hb, w1_ref[:, :D_MODEL], preferred_element_type=F32) + b1_ref[:, :D_MODEL]
        gt = jnp.dot(hb, w1_ref[:, D_MODEL:], preferred_element_type=F32) + b1_ref[:, D_MODEL:]
        u = a * _sigmoid(gt)
        if lo < HALO or hi > HALO + tm:
            row = lax.broadcasted_iota(jnp.int32, (hi - lo, 1), 0) + lo
            valid = (((row >= HALO) | (j > 0)) & ((row < HALO + tm) | (j < nj - 1)))
            u = jnp.where(valid, u, 0.0)
        for c in range(D_MODEL // LANES):
            buf_ref[c, lo:hi, :] = u[:, c * LANES:(c + 1) * LANES]

    def conv_group(r):
        r0 = r * CONV_ROWS
        base = HALO - CONV_PAD
        cols = []
        for c in range(D_MODEL // LANES):
            lanes = slice(c * LANES, (c + 1) * LANES)
            acc = jnp.zeros((CONV_ROWS, LANES), F32)
            for k in range(CONV_WIDTH):
                lo = r0 + base + k
                acc = acc + buf_ref[c, lo:lo + CONV_ROWS, :] * wdw_ref[k:k + 1, lanes]
            cols.append(acc + bdw_ref[:, lanes])
        v = jnp.concatenate(cols, axis=1)
        mu = jnp.mean(v, axis=-1, keepdims=True)
        vc = v - mu
        var = jnp.mean(vc * vc, axis=-1, keepdims=True)
        y = vc * lax.rsqrt(var + EPS) * lng_ref[...] + lnb_ref[...]
        y = y * _sigmoid(y)
        z = jnp.dot(y.astype(BF16), w2_ref[...], preferred_element_type=F32) + b2_ref[...]
        rows = slice(r0, r0 + CONV_ROWS)
        o_ref[0, rows, :] = x_ref[0, rows, :] + mod_ref[0, 2:3, :] * _rms(z, g_ref[1:2, :])

    bounds = [0] + [(r + 1) * CONV_ROWS + 2 * HALO for r in range(n_grp)]
    pointwise1(bounds[0], bounds[1])
    for r in range(n_grp):
        if r + 1 < n_grp:
            pointwise1(bounds[r + 1], bounds[r + 2])
        conv_group(r)


def _conv_ctx_kernel(*refs, tm):
    _conv_kernel(*refs[:13], *refs[14:], tm=tm)


def _conv(src, dst, mod_i, g_i, w1, b1, wdw, bdw, lng, lnb, w2, b2, *, layer, ctx):
    if ctx:
        tm = CTX_LEN
        grid = (BATCH, 1)
        blk = src.shape[0] - 1
        tile = lambda b, j: (blk, b, 0)
        prev = lambda b, j: (blk, 0, 0)
        nxt = lambda b, j: (blk, 0, 0)
        mod_map = lambda b, j: (CTX_BLK, 0, 0)
        out_tile = lambda b, j: (CTX_BLK, b, 0)
    else:
        tm = 512
        grid = (BATCH, SEQ // tm)
        hb = tm // HALO
        tile = lambda b, j: (b, j, 0)
        prev = lambda b, j: (b, jnp.maximum(j * hb - 1, 0), 0)
        nxt = lambda b, j: (b, jnp.minimum((j + 1) * hb, SEQ // HALO - 1), 0)
        mod_map = lambda b, j: (b, 0, 0)
        out_tile = tile
    in_specs = [
        pl.BlockSpec((1, tm, D_MODEL), tile),
        pl.BlockSpec((1, HALO, D_MODEL), prev),
        pl.BlockSpec((1, HALO, D_MODEL), nxt),
        pl.BlockSpec((1, N_MOD, D_MODEL), mod_map),
        _small((4, D_MODEL)),
        _resident((D_MODEL, 2 * D_MODEL), layer),
        _small((1, 2 * D_MODEL)),
        _small((CONV_WIDTH, D_MODEL)),
        _small((1, D_MODEL)),
        _small((1, D_MODEL)),
        _small((1, D_MODEL)),
        _resident((D_MODEL, D_MODEL), layer),
        _small((1, D_MODEL)),
    ]
    args = [src, src, src, mod_i, g_i, w1, b1, wdw, bdw, lng, lnb, w2, b2]
    if ctx:
        kern = functools.partial(_conv_ctx_kernel, tm=tm)
        in_specs.append(pl.BlockSpec(memory_space=pl.ANY))
        args.append(dst)
        aliases = {len(args) - 1: 0}
    else:
        kern = functools.partial(_conv_kernel, tm=tm)
        aliases = {}
    return pl.pallas_call(
        kern,
        grid=grid,
        in_specs=in_specs,
        out_specs=pl.BlockSpec((1, tm, D_MODEL), out_tile),
        out_shape=jax.ShapeDtypeStruct((N_STACK, SEQ, D_MODEL), F32),
        scratch_shapes=[
            pltpu.VMEM((tm + 2 * HALO, D_MODEL), BF16),
            pltpu.VMEM((D_MODEL // LANES, tm + 2 * HALO, LANES), F32),
        ],
        input_output_aliases=aliases,
        compiler_params=_params(("arbitrary", "arbitrary")),
        name="conv_ctx" if ctx else "conv",
    )(*args)


def _conv_ffn_kernel(x_ref, xp_ref, xn_ref, modc_ref, modf_ref, g_ref,
                     w1_ref, b1_ref, wdw_ref, bdw_ref, lng_ref, lnb_ref, w2_ref, b2_ref,
                     fw1_ref, fw3_ref, fw2_ref, o_ref, hb_ref, buf_ref, mid_ref,
                     *, tm, tiles_per_seq, n_sub, n_chunks):
    t = pl.program_id(0)
    n_tiles = pl.num_programs(0) - 1
    j = jnp.minimum(t, n_tiles - 1) % tiles_per_seq
    slot = t % 2
    win = tm + 2 * HALO
    n_grp = tm // CONV_ROWS

    @pl.when(t == 0)
    def _():
        mid_ref[1] = jnp.zeros((tm, D_MODEL), F32)

    def prenorm(x):
        h = _rms(x, g_ref[0:1, :]) * (1.0 + modc_ref[0, 1:2, :]) + modc_ref[0, 0:1, :]
        return h.astype(BF16)

    def conv_prologue():
        hb_ref[0:HALO, :] = prenorm(xp_ref[0])
        hb_ref[HALO:HALO + tm, :] = prenorm(x_ref[0])
        hb_ref[HALO + tm:win, :] = prenorm(xn_ref[0])

    def pointwise1(lo, hi):
        hb = hb_ref[lo:hi, :]
        a = jnp.dot(hb, w1_ref[:, :D_MODEL], preferred_element_type=F32) + b1_ref[:, :D_MODEL]
        gt = jnp.dot(hb, w1_ref[:, D_MODEL:], preferred_element_type=F32) + b1_ref[:, D_MODEL:]
        u = a * _sigmoid(gt)
        if lo < HALO or hi > HALO + tm:
            row = lax.broadcasted_iota(jnp.int32, (hi - lo, 1), 0) + lo
            valid = (((row >= HALO) | (j > 0)) & ((row < HALO + tm) | (j < tiles_per_seq - 1)))
            u = jnp.where(valid, u, 0.0)
        for c in range(D_MODEL // LANES):
            buf_ref[c, lo:hi, :] = u[:, c * LANES:(c + 1) * LANES]

    ybs = {}

    def conv_group(r):
        r0 = r * CONV_ROWS
        base = HALO - CONV_PAD
        cols = []
        for c in range(D_MODEL // LANES):
            lanes = slice(c * LANES, (c + 1) * LANES)
            acc = jnp.zeros((CONV_ROWS, LANES), F32)
            for k in range(CONV_WIDTH):
                lo = r0 + base + k
                acc = acc + buf_ref[c, lo:lo + CONV_ROWS, :] * wdw_ref[k:k + 1, lanes]
            cols.append(acc + bdw_ref[:, lanes])
        v = jnp.concatenate(cols, axis=1)
        mu = jnp.mean(v, axis=-1, keepdims=True)
        vc = v - mu
        var = jnp.mean(vc * vc, axis=-1, keepdims=True)
        y = vc * lax.rsqrt(var + EPS) * lng_ref[...] + lnb_ref[...]
        ybs[r] = (y * _sigmoid(y)).astype(BF16)

    def pointwise2(r):
        z = jnp.dot(ybs.pop(r), w2_ref[...], preferred_element_type=F32) + b2_ref[...]
        rows = slice(r * CONV_ROWS, (r + 1) * CONV_ROWS)
        mid_ref[slot, rows, :] = x_ref[0, rows, :] + modc_ref[0, 2:3, :] * _rms(z, g_ref[1:2, :])

    rows_sub = tm // n_sub
    fc = D_FF // n_chunks
    st = {"ab": {}}

    def ffn_prologue():
        st["x"] = [mid_ref[1 - slot, s * rows_sub:(s + 1) * rows_sub, :] for s in range(n_sub)]
        st["hb"] = [(_rms(x, g_ref[2:3, :]) * (1.0 + modf_ref[0, 4:5, :])
                     + modf_ref[0, 3:4, :]).astype(BF16) for x in st["x"]]
        st["acc"] = [None] * n_sub

    def ffn_up(c, s):
        cols = slice(c * fc, (c + 1) * fc)
        h = st["hb"][s]
        st["ab"][s] = (jnp.dot(h, fw1_ref[:, cols], preferred_element_type=F32),
                       jnp.dot(h, fw3_ref[:, cols], preferred_element_type=F32))

    def ffn_down(c, s):
        cols = slice(c * fc, (c + 1) * fc)
        a, b = st["ab"].pop(s)
        act = (a * _sigmoid(a) * b).astype(BF16)
        part = jnp.dot(act, fw2_ref[cols, :], preferred_element_type=F32)
        st["acc"][s] = part if st["acc"][s] is None else st["acc"][s] + part

    def ffn_epilogue():
        for s in range(n_sub):
            o_ref[0, s * rows_sub:(s + 1) * rows_sub, :] = (
                st["x"][s] + modf_ref[0, 5:6, :] * _rms(st["acc"][s], g_ref[3:4, :]))

    bounds = [0] + [(r + 1) * CONV_ROWS + 2 * HALO for r in range(n_grp)]
    ffn_steps = []
    for c in range(n_chunks):
        ffn_steps += [functools.partial(ffn_up, c, s) for s in range(n_sub)]
        ffn_steps += [functools.partial(ffn_down, c, s) for s in range(n_sub)]
    per_grp = len(ffn_steps) // n_grp
    conv_prologue()
    ffn_prologue()
    pointwise1(bounds[0], bounds[1])
    if n_grp > 1:
        pointwise1(bounds[1], bounds[2])
    for r in range(n_grp):
        conv_group(r)
        if r + 2 < n_grp:
            pointwise1(bounds[r + 2], bounds[r + 3])
        for step in ffn_steps[r * per_grp:(r + 1) * per_grp]:
            step()
        pointwise2(r)
    for step in ffn_steps[n_grp * per_grp:]:
        step()
    ffn_epilogue()


def _conv_ffn(src, dst, mod_i, g_i, conv_p, ffn_p, *, conv_layer, layer, ctx):
    hb = None
    if ctx:
        tm, n_tiles, tiles_per_seq, n_sub = CTX_LEN, BATCH, 1, 1
        blk = src.shape[0] - 1
        tile = lambda t: (blk, jnp.minimum(t, n_tiles - 1), 0)
        prev = lambda t: (blk, 0, 0)
        nxt = lambda t: (blk, 0, 0)
        modc = lambda t: (CTX_BLK, 0, 0)
        modf = lambda t: (CTX_BLK, 0, 0)
        out_tile = lambda t: (CTX_BLK, jnp.maximum(t - 1, 0), 0)
    else:
        tm, n_sub = 512, 2
        tiles_per_seq = SEQ // tm
        n_tiles = BATCH * tiles_per_seq
        hb = tm // HALO

        def split(tile_idx):
            return tile_idx // tiles_per_seq, tile_idx % tiles_per_seq

        def tile(t):
            b, j = split(jnp.minimum(t, n_tiles - 1))
            return (b, j, 0)

        def prev(t):
            b, j = split(jnp.minimum(t, n_tiles - 1))
            return (b, jnp.maximum(j * hb - 1, 0), 0)

        def nxt(t):
            b, j = split(jnp.minimum(t, n_tiles - 1))
            return (b, jnp.minimum((j + 1) * hb, SEQ // HALO - 1), 0)

        def modc(t):
            return (split(jnp.minimum(t, n_tiles - 1))[0], 0, 0)

        def modf(t):
            return (split(jnp.maximum(t - 1, 0))[0], 0, 0)

        def out_tile(t):
            b, j = split(jnp.maximum(t - 1, 0))
            return (b, j, 0)

    w1, b1, wdw, bdw, lng, lnb, w2, b2 = conv_p
    fw1, fw3, fw2 = ffn_p
    in_specs = [
        pl.BlockSpec((1, tm, D_MODEL), tile),
        pl.BlockSpec((1, HALO, D_MODEL), prev),
        pl.BlockSpec((1, HALO, D_MODEL), nxt),
        pl.BlockSpec((1, N_MOD, D_MODEL), modc),
        pl.BlockSpec((1, N_MOD, D_MODEL), modf),
        _small((4, D_MODEL)),
        _resident((D_MODEL, 2 * D_MODEL), conv_layer),
        _small((1, 2 * D_MODEL)),
        _small((CONV_WIDTH, D_MODEL)),
        _small((1, D_MODEL)),
        _small((1, D_MODEL)),
        _small((1, D_MODEL)),
        _resident((D_MODEL, D_MODEL), conv_layer),
        _small((1, D_MODEL)),
        _resident((D_MODEL, D_FF), layer),
        _resident((D_MODEL, D_FF), layer),
        _resident((D_FF, D_MODEL), layer),
    ]
    args = [src, src, src, mod_i, mod_i, g_i, w1, b1, wdw, bdw, lng, lnb, w2, b2, fw1, fw3, fw2]
    kw = dict(tm=tm, tiles_per_seq=tiles_per_seq, n_sub=n_sub, n_chunks=2)
    if ctx:
        kern = functools.partial(_conv_ffn_ctx_kernel, **kw)
        in_specs.append(pl.BlockSpec(memory_space=pl.ANY))
        args.append(dst)
        aliases = {len(args) - 1: 0}
    else:
        kern = functools.partial(_conv_ffn_kernel, **kw)
        aliases = {}
    return pl.pallas_call(
        kern,
        grid=(n_tiles + 1,),
        in_specs=in_specs,
        out_specs=pl.BlockSpec((1, tm, D_MODEL), out_tile),
        out_shape=jax.ShapeDtypeStruct((N_STACK, SEQ, D_MODEL), F32),
        scratch_shapes=[
            pltpu.VMEM((tm + 2 * HALO, D_MODEL), BF16),
            pltpu.VMEM((D_MODEL // LANES, tm + 2 * HALO, LANES), F32),
            pltpu.VMEM((2, tm, D_MODEL), F32),
        ],
        input_output_aliases=aliases,
        compiler_params=_params(("arbitrary",)),
        name="conv_ffn_ctx" if ctx else "conv_ffn",
    )(*args)


def _conv_ffn_ctx_kernel(*refs, **kw):
    _conv_ffn_kernel(*refs[:17], *refs[18:], **kw)


HALF = HEAD_DIM // 2


def _head_norm_rope_t(xh, tab):
    xe, xo = xh[:HALF], xh[HALF:]
    ss = jnp.sum(xe * xe, axis=0, keepdims=True) + jnp.sum(xo * xo, axis=0, keepdims=True)
    rs = lax.rsqrt(ss * (1.0 / HEAD_DIM) + EPS)
    oe = (xe * tab[0:HALF] + xo * tab[HALF:2 * HALF]) * rs
    oo = (xe * tab[2 * HALF:3 * HALF] + xo * tab[3 * HALF:4 * HALF]) * rs
    return oe, oo


def _qkv_kernel(x_ref, mod_ref, g_ref, wt_ref, qtab_ref, ktab_ref, q_ref, k_ref, v_ref):
    x = x_ref[0]
    h = _rms(x, g_ref[0:1, :]) * (1.0 + mod_ref[0, 1:2, :]) + mod_ref[0, 0:1, :]
    nt = (((1,), (1,)), ((), ()))
    yt = lax.dot_general(wt_ref[...], h.astype(BF16), nt,
                         preferred_element_type=F32)
    qtab = qtab_ref[0]
    ktab = ktab_ref[0]
    for hd in range(N_HEADS):
        oe, oo = _head_norm_rope_t(yt[hd * HEAD_DIM:(hd + 1) * HEAD_DIM], qtab)
        q_ref[0, hd, 0:HALF, :] = oe.astype(BF16)
        q_ref[0, hd, HALF:HEAD_DIM, :] = oo.astype(BF16)
    for hd in range(N_KV_HEADS):
        lo = D_MODEL + hd * HEAD_DIM
        oe, oo = _head_norm_rope_t(yt[lo:lo + HEAD_DIM], ktab)
        k_ref[0, hd] = jnp.concatenate([oe, oo], axis=0).T.astype(BF16)
    row = lax.broadcasted_iota(jnp.int32, (V_ROWS - HEAD_DIM, x.shape[0]), 0)
    tail = jnp.where(row == 0, 1.0, 0.0).astype(BF16)
    for hd in range(N_KV_HEADS):
        lo = D_MODEL + KV_DIM + hd * HEAD_DIM
        v_ref[0, hd, 0:HEAD_DIM, :] = yt[lo:lo + HEAD_DIM].astype(BF16)
        v_ref[0, hd, HEAD_DIM:V_ROWS, :] = tail


def _qkv(xs, mod_i, g_i, wqkv_t, qtab, ktab, layer):
    tm = 512
    tab_spec = pl.BlockSpec((1, 4 * HALF, tm), lambda b, j: (b // BATCH, 0, j))
    return pl.pallas_call(
        _qkv_kernel,
        grid=(N_STACK, SEQ // tm),
        in_specs=[
            pl.BlockSpec((1, tm, D_MODEL), lambda b, j: (b, j, 0)),
            pl.BlockSpec((1, N_MOD, D_MODEL), lambda b, j: (b, 0, 0)),
            _small((4, D_MODEL)),
            _resident((QKV_DIM, D_MODEL), layer),
            tab_spec,
            tab_spec,
        ],
        out_specs=[
            pl.BlockSpec((1, N_HEADS, HEAD_DIM, tm), lambda b, j: (b, 0, 0, j)),
            pl.BlockSpec((1, N_KV_HEADS, tm, HEAD_DIM), lambda b, j: (b, 0, j, 0)),
            pl.BlockSpec((1, N_KV_HEADS, V_ROWS, tm), lambda b, j: (b, 0, 0, j)),
        ],
        out_shape=[
            jax.ShapeDtypeStruct((N_STACK, N_HEADS, HEAD_DIM, SEQ), BF16),
            jax.ShapeDtypeStruct((N_STACK, N_KV_HEADS, SEQ, HEAD_DIM), BF16),
            jax.ShapeDtypeStruct((N_STACK, N_KV_HEADS, V_ROWS, SEQ), BF16),
        ],
        compiler_params=_params(("arbitrary", "arbitrary")),
        name="attn_qkv",
    )(xs, mod_i, g_i, wqkv_t, qtab, ktab)


Q_COLS = 256


def _attn_kernel(*refs, n_src, tk, lag):
    q_ref = refs[0]
    kv_refs = refs[1:1 + 2 * n_src]
    o_ref = refs[-1]
    chunks = []
    for i in range(n_src):
        k_ref, vt_ref = kv_refs[2 * i], kv_refs[2 * i + 1]
        n_keys = k_ref.shape[2]
        step = min(tk, n_keys)
        chunks += [(k_ref, vt_ref, lo, step) for lo in range(0, n_keys, step)]
    tq = q_ref.shape[3]
    groups = [(h, qs) for qs in range(0, tq, Q_COLS) for h in range(KV_GROUP)]
    items = [(c, g) for c in chunks for g in range(len(groups))]
    m = [None] * len(groups)
    acc = [None] * len(groups)
    scores = {}
    for idx in range(len(items) + lag):
        if idx < len(items):
            (k_ref, _, lo, step), g = items[idx]
            h, qs = groups[g]
            scores[idx] = jnp.dot(k_ref[0, 0, lo:lo + step, :], q_ref[0, h, :, qs:qs + Q_COLS],
                                  preferred_element_type=F32)
        if idx >= lag:
            (_, vt_ref, lo, step), g = items[idx - lag]
            st = scores.pop(idx - lag)
            mj = st.max(axis=0, keepdims=True)
            m_new = mj if m[g] is None else jnp.maximum(m[g], mj)
            pt = jnp.exp2(st - m_new).astype(BF16)
            pv = jnp.dot(vt_ref[0, 0, :, lo:lo + step], pt, preferred_element_type=F32)
            acc[g] = pv if acc[g] is None else jnp.exp2(m[g] - m_new) * acc[g] + pv
            m[g] = m_new
    for qs in range(0, tq, Q_COLS):
        outs = [acc[g][:HEAD_DIM] / acc[g][HEAD_DIM:HEAD_DIM + 1]
                for g, (_, gq) in enumerate(groups) if gq == qs]
        o_ref[0, qs:qs + Q_COLS, :] = jnp.concatenate(outs, axis=0).T.astype(BF16)


def _attn_latent(q, k, vt):
    tq = 1024
    kern = functools.partial(_attn_kernel, n_src=2, tk=256, lag=6)
    k_lat = pl.BlockSpec((1, 1, SEQ, HEAD_DIM), lambda b, g, i: (b, g, 0, 0))
    k_ctx = pl.BlockSpec((1, 1, CTX_LEN, HEAD_DIM), lambda b, g, i: (CTX_BLK, g, b, 0))
    vt_lat = pl.BlockSpec((1, 1, V_ROWS, SEQ), lambda b, g, i: (b, g, 0, 0))
    vt_ctx = pl.BlockSpec((1, 1, V_ROWS, CTX_LEN), lambda b, g, i: (CTX_BLK, g, 0, b))
    return pl.pallas_call(
        kern,
        grid=(BATCH, N_KV_HEADS, SEQ // tq),
        in_specs=[
            pl.BlockSpec((1, KV_GROUP, HEAD_DIM, tq), lambda b, g, i: (b, g, 0, i)),
            k_lat, vt_lat, k_ctx, vt_ctx,
        ],
        out_specs=pl.BlockSpec((1, tq, KV_GROUP * HEAD_DIM), lambda b, g, i: (b, i, g)),
        out_shape=jax.ShapeDtypeStruct((N_STACK, SEQ, D_MODEL), BF16),
        compiler_params=_params(("arbitrary", "arbitrary", "arbitrary")),
        name="attn_latent",
    )(q, k, vt, k, vt)


def _attn_ctx_kernel(q_ref, k_ref, vt_ref, o_prev_ref, o_ref):
    del o_prev_ref
    _attn_kernel(q_ref, k_ref, vt_ref, o_ref, n_src=1, tk=CTX_LEN, lag=2)


def _attn_ctx(q, k, vt, o):
    return pl.pallas_call(
        _attn_ctx_kernel,
        grid=(BATCH, N_KV_HEADS),
        in_specs=[
            pl.BlockSpec((1, KV_GROUP, HEAD_DIM, CTX_LEN), lambda b, g: (CTX_BLK, g, 0, b)),
            pl.BlockSpec((1, 1, CTX_LEN, HEAD_DIM), lambda b, g: (CTX_BLK, g, b, 0)),
            pl.BlockSpec((1, 1, V_ROWS, CTX_LEN), lambda b, g: (CTX_BLK, g, 0, b)),
            pl.BlockSpec(memory_space=pl.ANY),
        ],
        out_specs=pl.BlockSpec((1, CTX_LEN, KV_GROUP * HEAD_DIM), lambda b, g: (CTX_BLK, b, g)),
        out_shape=jax.ShapeDtypeStruct((N_STACK, SEQ, D_MODEL), BF16),
        input_output_aliases={3: 0},
        compiler_params=_params(("arbitrary", "arbitrary")),
        name="attn_ctx",
    )(q, k, vt, o)


def _oproj_kernel(o_ref, x_ref, mod_ref, g_ref, w_ref, out_ref):
    y = jnp.dot(o_ref[0], w_ref[...], preferred_element_type=F32)
    out_ref[0] = x_ref[0] + mod_ref[0, 2:3, :] * _rms(y, g_ref[1:2, :])


def _oproj(o, xs, mod_i, g_i, wo, layer, n_blk):
    tm = 512
    return pl.pallas_call(
        _oproj_kernel,
        grid=(n_blk, SEQ // tm),
        in_specs=[
            pl.BlockSpec((1, tm, D_MODEL), lambda b, j: (b, j, 0)),
            pl.BlockSpec((1, tm, D_MODEL), lambda b, j: (b, j, 0)),
            pl.BlockSpec((1, N_MOD, D_MODEL), lambda b, j: (b, 0, 0)),
            _small((4, D_MODEL)),
            _resident((D_MODEL, D_MODEL), layer),
        ],
        out_specs=pl.BlockSpec((1, tm, D_MODEL), lambda b, j: (b, j, 0)),
        out_shape=jax.ShapeDtypeStruct((N_STACK, SEQ, D_MODEL), F32),
        input_output_aliases={1: 0},
        compiler_params=_params(("arbitrary", "arbitrary")),
        name="attn_oproj",
    )(o, xs, mod_i, g_i, wo)


def _rope_tables():
    t = np.arange(SEQ)
    row = (t // GRID_W).astype(np.float64)
    col = (t % GRID_W).astype(np.float64)
    freqs = ROPE_THETA ** (-np.arange(ROPE_PAIRS_PER_AXIS, dtype=np.float64) / ROPE_PAIRS_PER_AXIS)
    ang = np.concatenate([row[:, None] * freqs[None, :], col[:, None] * freqs[None, :]], axis=-1)
    cos = np.cos(ang).T
    sin = np.sin(ang).T
    cos_t = np.stack([cos, np.ones_like(cos)]).astype(np.float32)
    sin_t = np.stack([sin, np.zeros_like(sin)]).astype(np.float32)
    return jnp.asarray(cos_t), jnp.asarray(sin_t)


def _norm_rope_table(g, cos_t, sin_t, scale):
    ge = g[0::2][None, :, None] * scale
    go = g[1::2][None, :, None] * scale
    return jnp.concatenate([ge * cos_t, -(go * sin_t), ge * sin_t, go * cos_t], axis=1)


def _even_odd_columns(n_heads):
    d = np.arange(HEAD_DIM)
    within = np.concatenate([d[0::2], d[1::2]])
    return (np.arange(n_heads)[:, None] * HEAD_DIM + within[None, :]).reshape(-1)


def kernel(x, c, ctx, c_ctx, w_mod, b_mod, norm_g, conv_w_pw1, conv_b_pw1, conv_w_dw, conv_b_dw, conv_ln_g, conv_ln_b, conv_w_pw2, conv_b_pw2, attn_wq, attn_wk, attn_wv, attn_wo, attn_q_g, attn_k_g, ffn_w1, ffn_w3, ffn_w2):
    assert x.shape == (BATCH, SEQ, D_MODEL) and ctx.shape == (BATCH, CTX_LEN, D_MODEL)
    xs = None
    cc = jnp.concatenate([c, c_ctx[None, :],
                          jnp.zeros((MOD_ROWS - N_STACK, D_MODEL), F32)], axis=0)
    mod = _mod_table(cc, w_mod, b_mod)[:, :N_STACK].reshape(DEPTH, N_STACK, N_MOD, D_MODEL)
    cos_t, sin_t = _rope_tables()

    w_pw1 = conv_w_pw1.astype(BF16)
    w_pw2 = conv_w_pw2.astype(BF16)
    wqkv_t = jnp.concatenate([attn_wq.astype(BF16)[:, :, _even_odd_columns(N_HEADS)],
                              attn_wk.astype(BF16)[:, :, _even_odd_columns(N_KV_HEADS)],
                              attn_wv.astype(BF16)], axis=2).transpose(0, 2, 1)
    wo = attn_wo.astype(BF16)
    w1 = ffn_w1.astype(BF16)
    w3 = ffn_w3.astype(BF16)
    w2 = ffn_w2.astype(BF16)

    for i in range(DEPTH):
        need_ctx = i < DEPTH - 1
        n_blk = N_STACK if need_ctx else BATCH
        mod_i = mod[i]
        g_i = norm_g[i]
        j = i // 2
        if i % 2 == 0:
            assert need_ctx
            cp = (w_pw1, conv_b_pw1[j][None, :], conv_w_dw[j], conv_b_dw[j][None, :],
                  conv_ln_g[j][None, :], conv_ln_b[j][None, :], w_pw2, conv_b_pw2[j][None, :])
            src_x, src_c = (x, ctx.reshape(1, BATCH * CTX_LEN, D_MODEL)) if i == 0 else (xs, xs)
            new = _conv(src_x, None, mod_i, g_i, *cp, layer=j, ctx=False)
            xs = _conv(src_c, new, mod_i, g_i, *cp, layer=j, ctx=True)
        else:
            qtab = _norm_rope_table(attn_q_g[j], cos_t, sin_t, LOG2_E * HEAD_DIM ** -0.5)
            ktab = _norm_rope_table(attn_k_g[j], cos_t, sin_t, 1.0)
            q, k, v = _qkv(xs, mod_i, g_i, wqkv_t, qtab, ktab, j)
            o = _attn_latent(q, k, v)
            if need_ctx:
                o = _attn_ctx(q, k, v, o)
            xs = _ffn(xs, mod_i, g_i, w1, w3, w2, i, n_blk, in_place=need_ctx, attn=(o, wo, j))
            continue
        xs = _ffn(xs, mod_i, g_i, w1, w3, w2, i, n_blk, in_place=need_ctx)
    return xs
```

```python
import functools

import numpy as np
import jax
import jax.numpy as jnp
from jax import lax
from jax.experimental import pallas as pl
from jax.experimental.pallas import tpu as pltpu

D_MODEL = 1024
BATCH = 8
SEQ = 2048
DEPTH = 4
CTX_LEN = 256
GRID_W = 64
N_HEADS = 16
N_KV_HEADS = 4
HEAD_DIM = 64
KV_GROUP = N_HEADS // N_KV_HEADS
ROPE_PAIRS_PER_AXIS = HEAD_DIM // 4
ROPE_THETA = 10000.0
CONV_WIDTH = 31
CONV_PAD = CONV_WIDTH // 2
D_FF = 2816
N_MOD = 6
EPS = 1e-6
LOG2_E = 1.4426950408889634

N_STACK = BATCH + 1
CTX_BLK = BATCH
KV_DIM = N_KV_HEADS * HEAD_DIM
QKV_DIM = D_MODEL + 2 * KV_DIM
LANES = 128
HALO = 16
MOD_ROWS = 16
V_ROWS = HEAD_DIM + 16
VMEM_LIMIT = 56 * 1024 * 1024

BF16 = jnp.bfloat16
F32 = jnp.float32


def _params(sem):
    return pltpu.CompilerParams(dimension_semantics=sem, vmem_limit_bytes=VMEM_LIMIT)


def _rms(x, g):
    return x * lax.rsqrt(jnp.mean(x * x, axis=-1, keepdims=True) + EPS) * g


def _sigmoid(x):
    return 1.0 / (1.0 + jnp.exp(-x))


def _resident(shape, layer):
    nd = len(shape)
    return pl.BlockSpec((None,) + shape, lambda *_: (layer,) + (0,) * nd,
                        pipeline_mode=pl.Buffered(1))


def _small(shape):
    nd = len(shape)
    return pl.BlockSpec(shape, lambda *_: (0,) * nd)


def _mod_kernel(cc_ref, w_ref, b_ref, o_ref):
    s = cc_ref[...]
    s = s * _sigmoid(s)
    o_ref[0] = jnp.dot(s.astype(BF16), w_ref[0].astype(BF16),
                       preferred_element_type=F32) + b_ref[0]


def _mod_table(cc, w_mod, b_mod):
    tn = 1536
    n_out = N_MOD * D_MODEL
    return pl.pallas_call(
        _mod_kernel,
        grid=(DEPTH, n_out // tn),
        in_specs=[
            pl.BlockSpec((MOD_ROWS, D_MODEL), lambda i, j: (0, 0)),
            pl.BlockSpec((1, D_MODEL, tn), lambda i, j: (i, 0, j)),
            pl.BlockSpec((1, 1, tn), lambda i, j: (i, 0, j)),
        ],
        out_specs=pl.BlockSpec((1, MOD_ROWS, tn), lambda i, j: (i, 0, j)),
        out_shape=jax.ShapeDtypeStruct((DEPTH, MOD_ROWS, n_out), F32),
        compiler_params=_params(("arbitrary", "arbitrary")),
        name="mod_table",
    )(cc, w_mod, b_mod.reshape(DEPTH, 1, n_out))


def _ffn_kernel(x_ref, mod_ref, g_ref, w1_ref, w3_ref, w2_ref, *rest, n_chunks, n_sub):
    *attn, o_ref = rest
    rows = x_ref.shape[1] // n_sub
    fc = D_FF // n_chunks
    xs = [x_ref[0, s * rows:(s + 1) * rows, :] for s in range(n_sub)]
    if attn:
        attn_ref, wo_ref = attn
        ys = [jnp.dot(attn_ref[0, s * rows:(s + 1) * rows, :], wo_ref[...],
                      preferred_element_type=F32) for s in range(n_sub)]
        xs = [x + mod_ref[0, 2:3, :] * _rms(y, g_ref[1:2, :]) for x, y in zip(xs, ys)]
    hb = [(_rms(x, g_ref[2:3, :]) * (1.0 + mod_ref[0, 4:5, :]) + mod_ref[0, 3:4, :]).astype(BF16)
          for x in xs]
    acc = [None] * n_sub
    for c in range(n_chunks):
        cols = slice(c * fc, (c + 1) * fc)
        ab = [(jnp.dot(h, w1_ref[:, cols], preferred_element_type=F32),
               jnp.dot(h, w3_ref[:, cols], preferred_element_type=F32)) for h in hb]
        for s, (a, b) in enumerate(ab):
            act = (a * _sigmoid(a) * b).astype(BF16)
            part = jnp.dot(act, w2_ref[cols, :], preferred_element_type=F32)
            acc[s] = part if acc[s] is None else acc[s] + part
    for s in range(n_sub):
        o_ref[0, s * rows:(s + 1) * rows, :] = (
            xs[s] + mod_ref[0, 5:6, :] * _rms(acc[s], g_ref[3:4, :]))


def _ffn(xs, mod_i, g_i, w1, w3, w2, layer, n_blk, in_place, attn=None):
    tm = 1024
    kern = functools.partial(_ffn_kernel, n_chunks=2, n_sub=4)
    out_rows = N_STACK if in_place else n_blk
    tile = pl.BlockSpec((1, tm, D_MODEL), lambda b, j: (b, j, 0))
    in_specs = [
        tile,
        pl.BlockSpec((1, N_MOD, D_MODEL), lambda b, j: (b, 0, 0)),
        _small((4, D_MODEL)),
        _resident((D_MODEL, D_FF), layer),
        _resident((D_MODEL, D_FF), layer),
        _resident((D_FF, D_MODEL), layer),
    ]
    args = [xs, mod_i, g_i, w1, w3, w2]
    if attn is not None:
        o, wo, attn_layer = attn
        in_specs += [tile, _resident((D_MODEL, D_MODEL), attn_layer)]
        args += [o, wo]
    return pl.pallas_call(
        kern,
        grid=(n_blk, SEQ // tm),
        in_specs=in_specs,
        out_specs=tile,
        out_shape=jax.ShapeDtypeStruct((out_rows, SEQ, D_MODEL), F32),
        input_output_aliases={0: 0} if in_place else {},
        compiler_params=_params(("arbitrary", "arbitrary")),
        name="ffn" if attn is None else "oproj_ffn",
    )(*args)


CONV_ROWS = 128


def _conv_kernel(x_ref, xp_ref, xn_ref, mod_ref, g_ref, w1_ref, b1_ref, wdw_ref, bdw_ref,
                 lng_ref, lnb_ref, w2_ref, b2_ref, o_ref, hb_ref, buf_ref, *, tm):
    j = pl.program_id(1)
    nj = pl.num_programs(1)
    win = tm + 2 * HALO
    n_grp = tm // CONV_ROWS

    def prenorm(x):
        h = _rms(x, g_ref[0:1, :]) * (1.0 + mod_ref[0, 1:2, :]) + mod_ref[0, 0:1, :]
        return h.astype(BF16)

    hb_ref[0:HALO, :] = prenorm(xp_ref[0])
    hb_ref[HALO:HALO + tm, :] = prenorm(x_ref[0])
    hb_ref[HALO + tm:win, :] = prenorm(xn_ref[0])

    def pointwise1(lo, hi):
        hb = hb_ref[lo:hi, :]
        a = jnp.dot(hb, w1_ref[:, :D_MODEL], preferred_element_type=F32) + b1_ref[:, :D_MODEL]
        gt = jnp.dot(hb, w1_ref[:, D_MODEL:], preferred_element_type=F32) + b1_ref[:, D_MODEL:]
        u = a * _sigmoid(gt)
        if lo < HALO or hi > HALO + tm:
            row = lax.broadcasted_iota(jnp.int32, (hi - lo, 1), 0) + lo
            valid = (((row >= HALO) | (j > 0)) & ((row < HALO + tm) | (j < nj - 1)))
            u = jnp.where(valid, u, 0.0)
        for c in range(D_MODEL // LANES):
            buf_ref[c, lo:hi, :] = u[:, c * LANES:(c + 1) * LANES]

    def conv_group(r):
        r0 = r * CONV_ROWS
        base = HALO - CONV_PAD
        cols = []
        for c in range(D_MODEL // LANES):
            lanes = slice(c * LANES, (c + 1) * LANES)
            acc = jnp.zeros((CONV_ROWS, LANES), F32)
            for k in range(CONV_WIDTH):
                lo = r0 + base + k
                acc = acc + buf_ref[c, lo:lo + CONV_ROWS, :] * wdw_ref[k:k + 1, lanes]
            cols.append(acc + bdw_ref[:, lanes])
        v = jnp.concatenate(cols, axis=1)
        mu = jnp.mean(v, axis=-1, keepdims=True)
        vc = v - mu
        var = jnp.mean(vc * vc, axis=-1, keepdims=True)
        y = vc * lax.rsqrt(var + EPS) * lng_ref[...] + lnb_ref[...]
        y = y * _sigmoid(y)
        z = jnp.dot(y.astype(BF16), w2_ref[...], preferred_element_type=F32) + b2_ref[...]
        rows = slice(r0, r0 + CONV_ROWS)
        o_ref[0, rows, :] = x_ref[0, rows, :] + mod_ref[0, 2:3, :] * _rms(z, g_ref[1:2, :])

    bounds = [0] + [(r + 1) * CONV_ROWS + 2 * HALO for r in range(n_grp)]
    pointwise1(bounds[0], bounds[1])
    for r in range(n_grp):
        if r + 1 < n_grp:
            pointwise1(bounds[r + 1], bounds[r + 2])
        conv_group(r)


def _conv_ctx_kernel(*refs, tm):
    _conv_kernel(*refs[:13], *refs[14:], tm=tm)


def _conv_latent_kernel(*refs, tm):
    b = pl.program_id(0)

    @pl.when(b < BATCH)
    def _():
        _conv_kernel(*refs[:13], *refs[14:], tm=tm)

    @pl.when(b == BATCH)
    def _():
        refs[14][0] = refs[13][0]


def _conv(src, dst, mod_i, g_i, w1, b1, wdw, bdw, lng, lnb, w2, b2, *, layer, ctx):
    if ctx:
        tm = CTX_LEN
        grid = (BATCH, 1)
        blk = src.shape[0] - 1
        tile = lambda b, j: (blk, b, 0)
        prev = lambda b, j: (blk, 0, 0)
        nxt = lambda b, j: (blk, 0, 0)
        mod_map = lambda b, j: (CTX_BLK, 0, 0)
        out_tile = lambda b, j: (CTX_BLK, b, 0)
    else:
        tm = 512
        grid = (N_STACK, SEQ // tm)
        hb = tm // HALO
        lat = lambda b: jnp.minimum(b, BATCH - 1)
        tile = lambda b, j: (lat(b), j, 0)
        prev = lambda b, j: (lat(b), jnp.maximum(j * hb - 1, 0), 0)
        nxt = lambda b, j: (lat(b), jnp.minimum((j + 1) * hb, SEQ // HALO - 1), 0)
        mod_map = lambda b, j: (lat(b), 0, 0)
        out_tile = lambda b, j: (b, j, 0)
        ctx_blk = dst.shape[0] - 1
        ctx_tile = lambda b, j: (ctx_blk, jnp.where(b == BATCH, j, 0), 0)
    in_specs = [
        pl.BlockSpec((1, tm, D_MODEL), tile),
        pl.BlockSpec((1, HALO, D_MODEL), prev),
        pl.BlockSpec((1, HALO, D_MODEL), nxt),
        pl.BlockSpec((1, N_MOD, D_MODEL), mod_map),
        _small((4, D_MODEL)),
        _resident((D_MODEL, 2 * D_MODEL), layer),
        _small((1, 2 * D_MODEL)),
        _small((CONV_WIDTH, D_MODEL)),
        _small((1, D_MODEL)),
        _small((1, D_MODEL)),
        _small((1, D_MODEL)),
        _resident((D_MODEL, D_MODEL), layer),
        _small((1, D_MODEL)),
    ]
    args = [src, src, src, mod_i, g_i, w1, b1, wdw, bdw, lng, lnb, w2, b2]
    if ctx:
        kern = functools.partial(_conv_ctx_kernel, tm=tm)
        in_specs.append(pl.BlockSpec(memory_space=pl.ANY))
        args.append(dst)
        aliases = {len(args) - 1: 0}
    else:
        kern = functools.partial(_conv_latent_kernel, tm=tm)
        in_specs.append(pl.BlockSpec((1, tm, D_MODEL), ctx_tile))
        args.append(dst)
        aliases = {}
    return pl.pallas_call(
        kern,
        grid=grid,
        in_specs=in_specs,
        out_specs=pl.BlockSpec((1, tm, D_MODEL), out_tile),
        out_shape=jax.ShapeDtypeStruct((N_STACK, SEQ, D_MODEL), F32),
        scratch_shapes=[
            pltpu.VMEM((tm + 2 * HALO, D_MODEL), BF16),
            pltpu.VMEM((D_MODEL // LANES, tm + 2 * HALO, LANES), F32),
        ],
        input_output_aliases=aliases,
        compiler_params=_params(("arbitrary", "arbitrary")),
        name="conv_ctx" if ctx else "conv",
    )(*args)


HALF = HEAD_DIM // 2


def _head_norm_rope_t(xh, tab):
    xe, xo = xh[:HALF], xh[HALF:]
    ss = jnp.sum(xe * xe, axis=0, keepdims=True) + jnp.sum(xo * xo, axis=0, keepdims=True)
    rs = lax.rsqrt(ss * (1.0 / HEAD_DIM) + EPS)
    oe = (xe * tab[0:HALF] + xo * tab[HALF:2 * HALF]) * rs
    oo = (xe * tab[2 * HALF:3 * HALF] + xo * tab[3 * HALF:4 * HALF]) * rs
    return oe, oo


def _qkv_kernel(x_ref, mod_ref, g_ref, wt_ref, qtab_ref, ktab_ref, q_ref, k_ref, v_ref):
    x = x_ref[0]
    h = _rms(x, g_ref[0:1, :]) * (1.0 + mod_ref[0, 1:2, :]) + mod_ref[0, 0:1, :]
    nt = (((1,), (1,)), ((), ()))
    yt = lax.dot_general(wt_ref[...], h.astype(BF16), nt,
                         preferred_element_type=F32)
    qtab = qtab_ref[0]
    ktab = ktab_ref[0]
    for hd in range(N_HEADS):
        oe, oo = _head_norm_rope_t(yt[hd * HEAD_DIM:(hd + 1) * HEAD_DIM], qtab)
        q_ref[0, hd, 0:HALF, :] = oe.astype(BF16)
        q_ref[0, hd, HALF:HEAD_DIM, :] = oo.astype(BF16)
    for hd in range(N_KV_HEADS):
        lo = D_MODEL + hd * HEAD_DIM
        oe, oo = _head_norm_rope_t(yt[lo:lo + HEAD_DIM], ktab)
        k_ref[0, hd] = jnp.concatenate([oe, oo], axis=0).T.astype(BF16)
    row = lax.broadcasted_iota(jnp.int32, (V_ROWS - HEAD_DIM, x.shape[0]), 0)
    tail = jnp.where(row == 0, 1.0, 0.0).astype(BF16)
    for hd in range(N_KV_HEADS):
        lo = D_MODEL + KV_DIM + hd * HEAD_DIM
        v_ref[0, hd, 0:HEAD_DIM, :] = yt[lo:lo + HEAD_DIM].astype(BF16)
        v_ref[0, hd, HEAD_DIM:V_ROWS, :] = tail


def _qkv(xs, mod_i, g_i, wqkv_t, qtab, ktab, layer):
    tm = 512
    tab_spec = pl.BlockSpec((1, 4 * HALF, tm), lambda b, j: (b // BATCH, 0, j))
    return pl.pallas_call(
        _qkv_kernel,
        grid=(N_STACK, SEQ // tm),
        in_specs=[
            pl.BlockSpec((1, tm, D_MODEL), lambda b, j: (b, j, 0)),
            pl.BlockSpec((1, N_MOD, D_MODEL), lambda b, j: (b, 0, 0)),
            _small((4, D_MODEL)),
            _resident((QKV_DIM, D_MODEL), layer),
            tab_spec,
            tab_spec,
        ],
        out_specs=[
            pl.BlockSpec((1, N_HEADS, HEAD_DIM, tm), lambda b, j: (b, 0, 0, j)),
            pl.BlockSpec((1, N_KV_HEADS, tm, HEAD_DIM), lambda b, j: (b, 0, j, 0)),
            pl.BlockSpec((1, N_KV_HEADS, V_ROWS, tm), lambda b, j: (b, 0, 0, j)),
        ],
        out_shape=[
            jax.ShapeDtypeStruct((N_STACK, N_HEADS, HEAD_DIM, SEQ), BF16),
            jax.ShapeDtypeStruct((N_STACK, N_KV_HEADS, SEQ, HEAD_DIM), BF16),
            jax.ShapeDtypeStruct((N_STACK, N_KV_HEADS, V_ROWS, SEQ), BF16),
        ],
        compiler_params=_params(("arbitrary", "arbitrary")),
        name="attn_qkv",
    )(xs, mod_i, g_i, wqkv_t, qtab, ktab)


Q_COLS = 256


def _attn_kernel(*refs, n_src, tk, lag):
    q_ref = refs[0]
    kv_refs = refs[1:1 + 2 * n_src]
    o_ref = refs[-1]
    chunks = []
    for i in range(n_src):
        k_ref, vt_ref = kv_refs[2 * i], kv_refs[2 * i + 1]
        n_keys = k_ref.shape[2]
        step = min(tk, n_keys)
        chunks += [(k_ref, vt_ref, lo, step) for lo in range(0, n_keys, step)]
    tq = q_ref.shape[3]
    groups = [(h, qs) for qs in range(0, tq, Q_COLS) for h in range(KV_GROUP)]
    items = [(c, g) for c in chunks for g in range(len(groups))]
    m = [None] * len(groups)
    acc = [None] * len(groups)
    scores = {}
    for idx in range(len(items) + lag):
        if idx < len(items):
            (k_ref, _, lo, step), g = items[idx]
            h, qs = groups[g]
            scores[idx] = jnp.dot(k_ref[0, 0, lo:lo + step, :], q_ref[0, h, :, qs:qs + Q_COLS],
                                  preferred_element_type=F32)
        if idx >= lag:
            (_, vt_ref, lo, step), g = items[idx - lag]
            st = scores.pop(idx - lag)
            mj = st.max(axis=0, keepdims=True)
            m_new = mj if m[g] is None else jnp.maximum(m[g], mj)
            pt = jnp.exp2(st - m_new).astype(BF16)
            pv = jnp.dot(vt_ref[0, 0, :, lo:lo + step], pt, preferred_element_type=F32)
            acc[g] = pv if acc[g] is None else jnp.exp2(m[g] - m_new) * acc[g] + pv
            m[g] = m_new
    for qs in range(0, tq, Q_COLS):
        outs = [acc[g][:HEAD_DIM] / acc[g][HEAD_DIM:HEAD_DIM + 1]
                for g, (_, gq) in enumerate(groups) if gq == qs]
        o_ref[0, qs:qs + Q_COLS, :] = jnp.concatenate(outs, axis=0).T.astype(BF16)


def _attn_latent_kernel(*refs, **kw):
    b = pl.program_id(0)

    @pl.when(b < BATCH)
    def _():
        _attn_kernel(*refs, **kw)

    @pl.when(b == BATCH)
    def _():
        refs[-1][...] = jnp.zeros(refs[-1].shape, refs[-1].dtype)


def _attn_latent(q, k, vt):
    tq = 1024
    kern = functools.partial(_attn_latent_kernel, n_src=2, tk=256, lag=6)
    seq = lambda b: jnp.minimum(b, BATCH - 1)
    k_lat = pl.BlockSpec((1, 1, SEQ, HEAD_DIM), lambda b, g, i: (b, g, 0, 0))
    k_ctx = pl.BlockSpec((1, 1, CTX_LEN, HEAD_DIM), lambda b, g, i: (CTX_BLK, g, seq(b), 0))
    vt_lat = pl.BlockSpec((1, 1, V_ROWS, SEQ), lambda b, g, i: (b, g, 0, 0))
    vt_ctx = pl.BlockSpec((1, 1, V_ROWS, CTX_LEN), lambda b, g, i: (CTX_BLK, g, 0, seq(b)))
    return pl.pallas_call(
        kern,
        grid=(N_STACK, N_KV_HEADS, SEQ // tq),
        in_specs=[
            pl.BlockSpec((1, KV_GROUP, HEAD_DIM, tq), lambda b, g, i: (b, g, 0, i)),
            k_lat, vt_lat, k_ctx, vt_ctx,
        ],
        out_specs=pl.BlockSpec((1, tq, KV_GROUP * HEAD_DIM), lambda b, g, i: (b, i, g)),
        out_shape=jax.ShapeDtypeStruct((N_STACK, SEQ, D_MODEL), BF16),
        compiler_params=_params(("arbitrary", "arbitrary", "arbitrary")),
        name="attn_latent",
    )(q, k, vt, k, vt)


def _attn_ctx_kernel(q_ref, k_ref, vt_ref, o_prev_ref, o_ref):
    del o_prev_ref
    _attn_kernel(q_ref, k_ref, vt_ref, o_ref, n_src=1, tk=CTX_LEN, lag=2)


def _attn_ctx(q, k, vt, o):
    return pl.pallas_call(
        _attn_ctx_kernel,
        grid=(BATCH, N_KV_HEADS),
        in_specs=[
            pl.BlockSpec((1, KV_GROUP, HEAD_DIM, CTX_LEN), lambda b, g: (CTX_BLK, g, 0, b)),
            pl.BlockSpec((1, 1, CTX_LEN, HEAD_DIM), lambda b, g: (CTX_BLK, g, b, 0)),
            pl.BlockSpec((1, 1, V_ROWS, CTX_LEN), lambda b, g: (CTX_BLK, g, 0, b)),
            pl.BlockSpec(memory_space=pl.ANY),
        ],
        out_specs=pl.BlockSpec((1, CTX_LEN, KV_GROUP * HEAD_DIM), lambda b, g: (CTX_BLK, b, g)),
        out_shape=jax.ShapeDtypeStruct((N_STACK, SEQ, D_MODEL), BF16),
        input_output_aliases={3: 0},
        compiler_params=_params(("arbitrary", "arbitrary")),
        name="attn_ctx",
    )(q, k, vt, o)


def _rope_tables():
    t = np.arange(SEQ)
    row = (t // GRID_W).astype(np.float64)
    col = (t % GRID_W).astype(np.float64)
    freqs = ROPE_THETA ** (-np.arange(ROPE_PAIRS_PER_AXIS, dtype=np.float64) / ROPE_PAIRS_PER_AXIS)
    ang = np.concatenate([row[:, None] * freqs[None, :], col[:, None] * freqs[None, :]], axis=-1)
    cos = np.cos(ang).T
    sin = np.sin(ang).T
    cos_t = np.stack([cos, np.ones_like(cos)]).astype(np.float32)
    sin_t = np.stack([sin, np.zeros_like(sin)]).astype(np.float32)
    return jnp.asarray(cos_t), jnp.asarray(sin_t)


def _norm_rope_table(g, cos_t, sin_t, scale):
    ge = g[0::2][None, :, None] * scale
    go = g[1::2][None, :, None] * scale
    return jnp.concatenate([ge * cos_t, -(go * sin_t), ge * sin_t, go * cos_t], axis=1)


def _even_odd_columns(n_heads):
    d = np.arange(HEAD_DIM)
    within = np.concatenate([d[0::2], d[1::2]])
    return (np.arange(n_heads)[:, None] * HEAD_DIM + within[None, :]).reshape(-1)


def kernel(x, c, ctx, c_ctx, w_mod, b_mod, norm_g, conv_w_pw1, conv_b_pw1, conv_w_dw, conv_b_dw, conv_ln_g, conv_ln_b, conv_w_pw2, conv_b_pw2, attn_wq, attn_wk, attn_wv, attn_wo, attn_q_g, attn_k_g, ffn_w1, ffn_w3, ffn_w2):
    assert x.shape == (BATCH, SEQ, D_MODEL) and ctx.shape == (BATCH, CTX_LEN, D_MODEL)
    xs = None
    cc = jnp.concatenate([c, c_ctx[None, :],
                          jnp.zeros((MOD_ROWS - N_STACK, D_MODEL), F32)], axis=0)
    mod = _mod_table(cc, w_mod, b_mod)[:, :N_STACK].reshape(DEPTH, N_STACK, N_MOD, D_MODEL)
    cos_t, sin_t = _rope_tables()

    w_pw1 = conv_w_pw1.astype(BF16)
    w_pw2 = conv_w_pw2.astype(BF16)
    wqkv_t = jnp.concatenate([attn_wq.astype(BF16)[:, :, _even_odd_columns(N_HEADS)],
                              attn_wk.astype(BF16)[:, :, _even_odd_columns(N_KV_HEADS)],
                              attn_wv.astype(BF16)], axis=2).transpose(0, 2, 1)
    wo = attn_wo.astype(BF16)
    w1 = ffn_w1.astype(BF16)
    w3 = ffn_w3.astype(BF16)
    w2 = ffn_w2.astype(BF16)

    for i in range(DEPTH):
        need_ctx = i < DEPTH - 1
        n_blk = N_STACK if need_ctx else BATCH
        mod_i = mod[i]
        g_i = norm_g[i]
        j = i // 2
        if i % 2 == 0:
            assert need_ctx
            cp = (w_pw1, conv_b_pw1[j][None, :], conv_w_dw[j], conv_b_dw[j][None, :],
                  conv_ln_g[j][None, :], conv_ln_b[j][None, :], w_pw2, conv_b_pw2[j][None, :])
            src_x, src_c = (x, ctx.reshape(1, BATCH * CTX_LEN, D_MODEL)) if i == 0 else (xs, xs)
            new = _conv(src_x, src_c, mod_i, g_i, *cp, layer=j, ctx=False)
            xs = _conv(src_c, new, mod_i, g_i, *cp, layer=j, ctx=True)
            xs = _ffn(xs, mod_i, g_i, w1, w3, w2, i, n_blk, in_place=need_ctx)
        else:
            qtab = _norm_rope_table(attn_q_g[j], cos_t, sin_t, LOG2_E * HEAD_DIM ** -0.5)
            ktab = _norm_rope_table(attn_k_g[j], cos_t, sin_t, 1.0)
            q, k, v = _qkv(xs, mod_i, g_i, wqkv_t, qtab, ktab, j)
            o = _attn_latent(q, k, v)
            if need_ctx:
                o = _attn_ctx(q, k, v, o)
            xs = _ffn(xs, mod_i, g_i, w1, w3, w2, i, n_blk, in_place=need_ctx, attn=(o, wo, j))
    return xs
```

```python
import functools

import numpy as np
import jax
import jax.numpy as jnp
from jax import lax
from jax.experimental import pallas as pl
from jax.experimental.pallas import tpu as pltpu

D_MODEL = 1024
BATCH = 8
SEQ = 2048
DEPTH = 4
CTX_LEN = 256
GRID_W = 64
N_HEADS = 16
N_KV_HEADS = 4
HEAD_DIM = 64
KV_GROUP = N_HEADS // N_KV_HEADS
ROPE_PAIRS_PER_AXIS = HEAD_DIM // 4
ROPE_THETA = 10000.0
CONV_WIDTH = 31
CONV_PAD = CONV_WIDTH // 2
D_FF = 2816
N_MOD = 6
EPS = 1e-6
LOG2_E = 1.4426950408889634

N_STACK = BATCH + 1
CTX_BLK = BATCH
KV_DIM = N_KV_HEADS * HEAD_DIM
QKV_DIM = D_MODEL + 2 * KV_DIM
LANES = 128
HALO = 16
MOD_ROWS = 16
V_ROWS = HEAD_DIM + 16
VMEM_LIMIT = 56 * 1024 * 1024

BF16 = jnp.bfloat16
F32 = jnp.float32


def _params(sem):
    return pltpu.CompilerParams(dimension_semantics=sem, vmem_limit_bytes=VMEM_LIMIT)


def _rms(x, g):
    return x * lax.rsqrt(jnp.mean(x * x, axis=-1, keepdims=True) + EPS) * g


def _sigmoid(x):
    return 1.0 / (1.0 + jnp.exp(-x))


def _resident(shape, layer):
    nd = len(shape)
    return pl.BlockSpec((None,) + shape, lambda *_: (layer,) + (0,) * nd,
                        pipeline_mode=pl.Buffered(1))


def _small(shape):
    nd = len(shape)
    return pl.BlockSpec(shape, lambda *_: (0,) * nd)


def _mod_kernel(cc_ref, w_ref, b_ref, o_ref):
    s = cc_ref[...]
    s = s * _sigmoid(s)
    o_ref[0] = jnp.dot(s.astype(BF16), w_ref[0].astype(BF16),
                       preferred_element_type=F32) + b_ref[0]


def _mod_table(cc, w_mod, b_mod):
    tn = 1536
    n_out = N_MOD * D_MODEL
    return pl.pallas_call(
        _mod_kernel,
        grid=(DEPTH, n_out // tn),
        in_specs=[
            pl.BlockSpec((MOD_ROWS, D_MODEL), lambda i, j: (0, 0)),
            pl.BlockSpec((1, D_MODEL, tn), lambda i, j: (i, 0, j)),
            pl.BlockSpec((1, 1, tn), lambda i, j: (i, 0, j)),
        ],
        out_specs=pl.BlockSpec((1, MOD_ROWS, tn), lambda i, j: (i, 0, j)),
        out_shape=jax.ShapeDtypeStruct((DEPTH, MOD_ROWS, n_out), F32),
        compiler_params=_params(("arbitrary", "arbitrary")),
        name="mod_table",
    )(cc, w_mod, b_mod.reshape(DEPTH, 1, n_out))


def _ffn_kernel(x_ref, mod_ref, g_ref, w1_ref, w3_ref, w2_ref, *rest, n_chunks, n_sub):
    *attn, o_ref = rest
    rows = x_ref.shape[1] // n_sub
    fc = D_FF // n_chunks
    xs = [x_ref[0, s * rows:(s + 1) * rows, :] for s in range(n_sub)]
    if attn:
        attn_ref, wo_ref = attn
        ys = [jnp.dot(attn_ref[0, s * rows:(s + 1) * rows, :], wo_ref[...],
                      preferred_element_type=F32) for s in range(n_sub)]
        xs = [x + mod_ref[0, 2:3, :] * _rms(y, g_ref[1:2, :]) for x, y in zip(xs, ys)]
    hb = [(_rms(x, g_ref[2:3, :]) * (1.0 + mod_ref[0, 4:5, :]) + mod_ref[0, 3:4, :]).astype(BF16)
          for x in xs]
    acc = [None] * n_sub
    for c in range(n_chunks):
        cols = slice(c * fc, (c + 1) * fc)
        ab = [(jnp.dot(h, w1_ref[:, cols], preferred_element_type=F32),
               jnp.dot(h, w3_ref[:, cols], preferred_element_type=F32)) for h in hb]
        for s, (a, b) in enumerate(ab):
            act = (a * _sigmoid(a) * b).astype(BF16)
            part = jnp.dot(act, w2_ref[cols, :], preferred_element_type=F32)
            acc[s] = part if acc[s] is None else acc[s] + part
    for s in range(n_sub):
        o_ref[0, s * rows:(s + 1) * rows, :] = (
            xs[s] + mod_ref[0, 5:6, :] * _rms(acc[s], g_ref[3:4, :]))


def _ffn(xs, mod_i, g_i, w1, w3, w2, layer, n_blk, in_place, attn=None):
    tm = 1024
    kern = functools.partial(_ffn_kernel, n_chunks=2, n_sub=4)
    out_rows = N_STACK if in_place else n_blk
    tile = pl.BlockSpec((1, tm, D_MODEL), lambda b, j: (b, j, 0))
    in_specs = [
        tile,
        pl.BlockSpec((1, N_MOD, D_MODEL), lambda b, j: (b, 0, 0)),
        _small((4, D_MODEL)),
        _resident((D_MODEL, D_FF), layer),
        _resident((D_MODEL, D_FF), layer),
        _resident((D_FF, D_MODEL), layer),
    ]
    args = [xs, mod_i, g_i, w1, w3, w2]
    if attn is not None:
        o, wo, attn_layer = attn
        in_specs += [tile, _resident((D_MODEL, D_MODEL), attn_layer)]
        args += [o, wo]
    return pl.pallas_call(
        kern,
        grid=(n_blk, SEQ // tm),
        in_specs=in_specs,
        out_specs=tile,
        out_shape=jax.ShapeDtypeStruct((out_rows, SEQ, D_MODEL), F32),
        input_output_aliases={0: 0} if in_place else {},
        compiler_params=_params(("arbitrary", "arbitrary")),
        name="ffn" if attn is None else "oproj_ffn",
    )(*args)


CONV_ROWS = 128


def _conv_kernel(x_ref, xp_ref, xn_ref, mod_ref, g_ref, w1_ref, b1_ref, wdw_ref, bdw_ref,
                 lng_ref, lnb_ref, w2_ref, b2_ref, o_ref, hb_ref, buf_ref, *, tm):
    j = pl.program_id(1)
    nj = pl.num_programs(1)
    win = tm + 2 * HALO
    n_grp = tm // CONV_ROWS

    def prenorm(x):
        h = _rms(x, g_ref[0:1, :]) * (1.0 + mod_ref[0, 1:2, :]) + mod_ref[0, 0:1, :]
        return h.astype(BF16)

    hb_ref[0:HALO, :] = prenorm(xp_ref[0])
    hb_ref[HALO:HALO + tm, :] = prenorm(x_ref[0])
    hb_ref[HALO + tm:win, :] = prenorm(xn_ref[0])

    def pointwise1(lo, hi):
        hb = hb_ref[lo:hi, :]
        a = jnp.dot(hb, w1_ref[:, :D_MODEL], preferred_element_type=F32) + b1_ref[:, :D_MODEL]
        gt = jnp.dot(hb, w1_ref[:, D_MODEL:], preferred_element_type=F32) + b1_ref[:, D_MODEL:]
        u = a * _sigmoid(gt)
        if lo < HALO or hi > HALO + tm:
            row = lax.broadcasted_iota(jnp.int32, (hi - lo, 1), 0) + lo
            valid = (((row >= HALO) | (j > 0)) & ((row < HALO + tm) | (j < nj - 1)))
            u = jnp.where(valid, u, 0.0)
        for c in range(D_MODEL // LANES):
            buf_ref[c, lo:hi, :] = u[:, c * LANES:(c + 1) * LANES]

    def conv_group(r):
        r0 = r * CONV_ROWS
        base = HALO - CONV_PAD
        cols = []
        for c in range(D_MODEL // LANES):
            lanes = slice(c * LANES, (c + 1) * LANES)
            acc = jnp.zeros((CONV_ROWS, LANES), F32)
            for k in range(CONV_WIDTH):
                lo = r0 + base + k
                acc = acc + buf_ref[c, lo:lo + CONV_ROWS, :] * wdw_ref[k:k + 1, lanes]
            cols.append(acc + bdw_ref[:, lanes])
        v = jnp.concatenate(cols, axis=1)
        mu = jnp.mean(v, axis=-1, keepdims=True)
        vc = v - mu
        var = jnp.mean(vc * vc, axis=-1, keepdims=True)
        y = vc * lax.rsqrt(var + EPS) * lng_ref[...] + lnb_ref[...]
        y = y * _sigmoid(y)
        z = jnp.dot(y.astype(BF16), w2_ref[...], preferred_element_type=F32) + b2_ref[...]
        rows = slice(r0, r0 + CONV_ROWS)
        o_ref[0, rows, :] = x_ref[0, rows, :] + mod_ref[0, 2:3, :] * _rms(z, g_ref[1:2, :])

    bounds = [0] + [(r + 1) * CONV_ROWS + 2 * HALO for r in range(n_grp)]
    pointwise1(bounds[0], bounds[1])
    for r in range(n_grp):
        if r + 1 < n_grp:
            pointwise1(bounds[r + 1], bounds[r + 2])
        conv_group(r)


def _conv_ctx_kernel(*refs, tm):
    _conv_kernel(*refs[:13], *refs[14:], tm=tm)


def _conv_latent_kernel(*refs, tm):
    b = pl.program_id(0)

    @pl.when(b < BATCH)
    def _():
        _conv_kernel(*refs[:13], *refs[14:], tm=tm)

    @pl.when(b == BATCH)
    def _():
        refs[14][0] = refs[13][0]


def _conv(src, dst, mod_i, g_i, w1, b1, wdw, bdw, lng, lnb, w2, b2, *, layer, ctx):
    if ctx:
        tm = CTX_LEN
        grid = (BATCH, 1)
        blk = src.shape[0] - 1
        tile = lambda b, j: (blk, b, 0)
        prev = lambda b, j: (blk, 0, 0)
        nxt = lambda b, j: (blk, 0, 0)
        mod_map = lambda b, j: (CTX_BLK, 0, 0)
        out_tile = lambda b, j: (CTX_BLK, b, 0)
    else:
        tm = 512
        grid = (N_STACK, SEQ // tm)
        hb = tm // HALO
        lat = lambda b: jnp.minimum(b, BATCH - 1)
        tile = lambda b, j: (lat(b), j, 0)
        prev = lambda b, j: (lat(b), jnp.maximum(j * hb - 1, 0), 0)
        nxt = lambda b, j: (lat(b), jnp.minimum((j + 1) * hb, SEQ // HALO - 1), 0)
        mod_map = lambda b, j: (lat(b), 0, 0)
        out_tile = lambda b, j: (b, j, 0)
        ctx_blk = dst.shape[0] - 1
        ctx_tile = lambda b, j: (ctx_blk, jnp.where(b == BATCH, j, 0), 0)
    in_specs = [
        pl.BlockSpec((1, tm, D_MODEL), tile),
        pl.BlockSpec((1, HALO, D_MODEL), prev),
        pl.BlockSpec((1, HALO, D_MODEL), nxt),
        pl.BlockSpec((1, N_MOD, D_MODEL), mod_map),
        _small((4, D_MODEL)),
        _resident((D_MODEL, 2 * D_MODEL), layer),
        _small((1, 2 * D_MODEL)),
        _small((CONV_WIDTH, D_MODEL)),
        _small((1, D_MODEL)),
        _small((1, D_MODEL)),
        _small((1, D_MODEL)),
        _resident((D_MODEL, D_MODEL), layer),
        _small((1, D_MODEL)),
    ]
    args = [src, src, src, mod_i, g_i, w1, b1, wdw, bdw, lng, lnb, w2, b2]
    if ctx:
        kern = functools.partial(_conv_ctx_kernel, tm=tm)
        in_specs.append(pl.BlockSpec(memory_space=pl.ANY))
        args.append(dst)
        aliases = {len(args) - 1: 0}
    else:
        kern = functools.partial(_conv_latent_kernel, tm=tm)
        in_specs.append(pl.BlockSpec((1, tm, D_MODEL), ctx_tile))
        args.append(dst)
        aliases = {}
    return pl.pallas_call(
        kern,
        grid=grid,
        in_specs=in_specs,
        out_specs=pl.BlockSpec((1, tm, D_MODEL), out_tile),
        out_shape=jax.ShapeDtypeStruct((N_STACK, SEQ, D_MODEL), F32),
        scratch_shapes=[
            pltpu.VMEM((tm + 2 * HALO, D_MODEL), BF16),
            pltpu.VMEM((D_MODEL // LANES, tm + 2 * HALO, LANES), F32),
        ],
        input_output_aliases=aliases,
        compiler_params=_params(("arbitrary", "arbitrary")),
        name="conv_ctx" if ctx else "conv",
    )(*args)


HALF = HEAD_DIM // 2


def _head_norm_rope_t(xh, tab):
    xe, xo = xh[:HALF], xh[HALF:]
    ss = jnp.sum(xe * xe, axis=0, keepdims=True) + jnp.sum(xo * xo, axis=0, keepdims=True)
    rs = lax.rsqrt(ss * (1.0 / HEAD_DIM) + EPS)
    oe = (xe * tab[0:HALF] + xo * tab[HALF:2 * HALF]) * rs
    oo = (xe * tab[2 * HALF:3 * HALF] + xo * tab[3 * HALF:4 * HALF]) * rs
    return oe, oo


def _qkv_kernel(x_ref, mod_ref, g_ref, wt_ref, qtab_ref, ktab_ref, q_ref, k_ref, v_ref):
    x = x_ref[0]
    h = _rms(x, g_ref[0:1, :]) * (1.0 + mod_ref[0, 1:2, :]) + mod_ref[0, 0:1, :]
    nt = (((1,), (1,)), ((), ()))
    yt = lax.dot_general(wt_ref[...], h.astype(BF16), nt,
                         preferred_element_type=F32)
    qtab = qtab_ref[0]
    ktab = ktab_ref[0]
    for hd in range(N_HEADS):
        oe, oo = _head_norm_rope_t(yt[hd * HEAD_DIM:(hd + 1) * HEAD_DIM], qtab)
        q_ref[0, hd, 0:HALF, :] = oe.astype(BF16)
        q_ref[0, hd, HALF:HEAD_DIM, :] = oo.astype(BF16)
    for hd in range(N_KV_HEADS):
        lo = D_MODEL + hd * HEAD_DIM
        oe, oo = _head_norm_rope_t(yt[lo:lo + HEAD_DIM], ktab)
        k_ref[0, hd] = jnp.concatenate([oe, oo], axis=0).T.astype(BF16)
    row = lax.broadcasted_iota(jnp.int32, (V_ROWS - HEAD_DIM, x.shape[0]), 0)
    tail = jnp.where(row == 0, 1.0, 0.0).astype(BF16)
    for hd in range(N_KV_HEADS):
        lo = D_MODEL + KV_DIM + hd * HEAD_DIM
        v_ref[0, hd, 0:HEAD_DIM, :] = yt[lo:lo + HEAD_DIM].astype(BF16)
        v_ref[0, hd, HEAD_DIM:V_ROWS, :] = tail


def _qkv(xs, mod_i, g_i, wqkv_t, qtab, ktab, layer):
    tm = 512
    tab_spec = pl.BlockSpec((1, 4 * HALF, tm), lambda b, j: (b // BATCH, 0, j))
    return pl.pallas_call(
        _qkv_kernel,
        grid=(N_STACK, SEQ // tm),
        in_specs=[
            pl.BlockSpec((1, tm, D_MODEL), lambda b, j: (b, j, 0)),
            pl.BlockSpec((1, N_MOD, D_MODEL), lambda b, j: (b, 0, 0)),
            _small((4, D_MODEL)),
            _resident((QKV_DIM, D_MODEL), layer),
            tab_spec,
            tab_spec,
        ],
        out_specs=[
            pl.BlockSpec((1, N_HEADS, HEAD_DIM, tm), lambda b, j: (b, 0, 0, j)),
            pl.BlockSpec((1, N_KV_HEADS, tm, HEAD_DIM), lambda b, j: (b, 0, j, 0)),
            pl.BlockSpec((1, N_KV_HEADS, V_ROWS, tm), lambda b, j: (b, 0, 0, j)),
        ],
        out_shape=[
            jax.ShapeDtypeStruct((N_STACK, N_HEADS, HEAD_DIM, SEQ), BF16),
            jax.ShapeDtypeStruct((N_STACK, N_KV_HEADS, SEQ, HEAD_DIM), BF16),
            jax.ShapeDtypeStruct((N_STACK, N_KV_HEADS, V_ROWS, SEQ), BF16),
        ],
        compiler_params=_params(("arbitrary", "arbitrary")),
        name="attn_qkv",
    )(xs, mod_i, g_i, wqkv_t, qtab, ktab)


Q_COLS = 256


def _attn_kernel(*refs, n_src, tk, lag):
    q_ref = refs[0]
    kv_refs = refs[1:1 + 2 * n_src]
    o_ref = refs[-1]
    chunks = []
    for i in range(n_src):
        k_ref, vt_ref = kv_refs[2 * i], kv_refs[2 * i + 1]
        n_keys = k_ref.shape[2]
        step = min(tk, n_keys)
        chunks += [(k_ref, vt_ref, lo, step) for lo in range(0, n_keys, step)]
    tq = q_ref.shape[3]
    groups = [(h, qs) for qs in range(0, tq, Q_COLS) for h in range(KV_GROUP)]
    items = [(c, g) for c in chunks for g in range(len(groups))]
    m = [None] * len(groups)
    acc = [None] * len(groups)
    scores = {}
    for idx in range(len(items) + lag):
        if idx < len(items):
            (k_ref, _, lo, step), g = items[idx]
            h, qs = groups[g]
            scores[idx] = jnp.dot(k_ref[0, 0, lo:lo + step, :], q_ref[0, h, :, qs:qs + Q_COLS],
                                  preferred_element_type=F32)
        if idx >= lag:
            (_, vt_ref, lo, step), g = items[idx - lag]
            st = scores.pop(idx - lag)
            mj = st.max(axis=0, keepdims=True)
            m_new = mj if m[g] is None else jnp.maximum(m[g], mj)
            pt = jnp.exp2(st - m_new).astype(BF16)
            pv = jnp.dot(vt_ref[0, 0, :, lo:lo + step], pt, preferred_element_type=F32)
            acc[g] = pv if acc[g] is None else jnp.exp2(m[g] - m_new) * acc[g] + pv
            m[g] = m_new
    for qs in range(0, tq, Q_COLS):
        outs = [acc[g][:HEAD_DIM] / acc[g][HEAD_DIM:HEAD_DIM + 1]
                for g, (_, gq) in enumerate(groups) if gq == qs]
        o_ref[0, qs:qs + Q_COLS, :] = jnp.concatenate(outs, axis=0).T.astype(BF16)


def _attn_latent_kernel(*refs, **kw):
    b = pl.program_id(0)

    @pl.when(b < BATCH)
    def _():
        _attn_kernel(*refs, **kw)

    @pl.when(b == BATCH)
    def _():
        refs[-1][...] = jnp.zeros(refs[-1].shape, refs[-1].dtype)


def _attn_latent(q, k, vt):
    tq = 1024
    kern = functools.partial(_attn_latent_kernel, n_src=2, tk=256, lag=6)
    seq = lambda b: jnp.minimum(b, BATCH - 1)
    k_lat = pl.BlockSpec((1, 1, SEQ, HEAD_DIM), lambda b, g, i: (b, g, 0, 0))
    k_ctx = pl.BlockSpec((1, 1, CTX_LEN, HEAD_DIM), lambda b, g, i: (CTX_BLK, g, seq(b), 0))
    vt_lat = pl.BlockSpec((1, 1, V_ROWS, SEQ), lambda b, g, i: (b, g, 0, 0))
    vt_ctx = pl.BlockSpec((1, 1, V_ROWS, CTX_LEN), lambda b, g, i: (CTX_BLK, g, 0, seq(b)))
    return pl.pallas_call(
        kern,
        grid=(N_STACK, N_KV_HEADS, SEQ // tq),
        in_specs=[
            pl.BlockSpec((1, KV_GROUP, HEAD_DIM, tq), lambda b, g, i: (b, g, 0, i)),
            k_lat, vt_lat, k_ctx, vt_ctx,
        ],
        out_specs=pl.BlockSpec((1, tq, KV_GROUP * HEAD_DIM), lambda b, g, i: (b, i, g)),
        out_shape=jax.ShapeDtypeStruct((N_STACK, SEQ, D_MODEL), BF16),
        compiler_params=_params(("arbitrary", "arbitrary", "arbitrary")),
        name="attn_latent",
    )(q, k, vt, k, vt)


def _attn_ctx_kernel(q_ref, k_ref, vt_ref, o_prev_ref, o_ref):
    del o_prev_ref
    _attn_kernel(q_ref, k_ref, vt_ref, o_ref, n_src=1, tk=CTX_LEN, lag=2)


def _attn_ctx(q, k, vt, o):
    return pl.pallas_call(
        _attn_ctx_kernel,
        grid=(BATCH, N_KV_HEADS),
        in_specs=[
            pl.BlockSpec((1, KV_GROUP, HEAD_DIM, CTX_LEN), lambda b, g: (CTX_BLK, g, 0, b)),
            pl.BlockSpec((1, 1, CTX_LEN, HEAD_DIM), lambda b, g: (CTX_BLK, g, b, 0)),
            pl.BlockSpec((1, 1, V_ROWS, CTX_LEN), lambda b, g: (CTX_BLK, g, 0, b)),
            pl.BlockSpec(memory_space=pl.ANY),
        ],
        out_specs=pl.BlockSpec((1, CTX_LEN, KV_GROUP * HEAD_DIM), lambda b, g: (CTX_BLK, b, g)),
        out_shape=jax.ShapeDtypeStruct((N_STACK, SEQ, D_MODEL), BF16),
        input_output_aliases={3: 0},
        compiler_params=_params(("arbitrary", "arbitrary")),
        name="attn_ctx",
    )(q, k, vt, o)


def _rope_tables():
    t = np.arange(SEQ)
    row = (t // GRID_W).astype(np.float64)
    col = (t % GRID_W).astype(np.float64)
    freqs = ROPE_THETA ** (-np.arange(ROPE_PAIRS_PER_AXIS, dtype=np.float64) / ROPE_PAIRS_PER_AXIS)
    ang = np.concatenate([row[:, None] * freqs[None, :], col[:, None] * freqs[None, :]], axis=-1)
    cos = np.cos(ang).T
    sin = np.sin(ang).T
    cos_t = np.stack([cos, np.ones_like(cos)]).astype(np.float32)
    sin_t = np.stack([sin, np.zeros_like(sin)]).astype(np.float32)
    return jnp.asarray(cos_t), jnp.asarray(sin_t)


def _norm_rope_table(g, cos_t, sin_t, scale):
    ge = g[0::2][None, :, None] * scale
    go = g[1::2][None, :, None] * scale
    return jnp.concatenate([ge * cos_t, -(go * sin_t), ge * sin_t, go * cos_t], axis=1)


def _even_odd_transposed(w, n_heads):
    n = w.shape[0]
    w = w.reshape(n, D_MODEL, n_heads, HALF, 2).transpose(0, 2, 4, 3, 1)
    return w.reshape(n, n_heads * HEAD_DIM, D_MODEL)


def kernel(x, c, ctx, c_ctx, w_mod, b_mod, norm_g, conv_w_pw1, conv_b_pw1, conv_w_dw, conv_b_dw, conv_ln_g, conv_ln_b, conv_w_pw2, conv_b_pw2, attn_wq, attn_wk, attn_wv, attn_wo, attn_q_g, attn_k_g, ffn_w1, ffn_w3, ffn_w2):
    assert x.shape == (BATCH, SEQ, D_MODEL) and ctx.shape == (BATCH, CTX_LEN, D_MODEL)
    xs = None
    cc = jnp.concatenate([c, c_ctx[None, :],
                          jnp.zeros((MOD_ROWS - N_STACK, D_MODEL), F32)], axis=0)
    mod = _mod_table(cc, w_mod, b_mod)[:, :N_STACK].reshape(DEPTH, N_STACK, N_MOD, D_MODEL)
    cos_t, sin_t = _rope_tables()

    w_pw1 = conv_w_pw1.astype(BF16)
    w_pw2 = conv_w_pw2.astype(BF16)
    wqkv_t = jnp.concatenate([_even_odd_transposed(attn_wq, N_HEADS),
                              _even_odd_transposed(attn_wk, N_KV_HEADS),
                              attn_wv.transpose(0, 2, 1)], axis=1).astype(BF16)
    wo = attn_wo.astype(BF16)
    w1 = ffn_w1.astype(BF16)
    w3 = ffn_w3.astype(BF16)
    w2 = ffn_w2.astype(BF16)

    for i in range(DEPTH):
        need_ctx = i < DEPTH - 1
        n_blk = N_STACK if need_ctx else BATCH
        mod_i = mod[i]
        g_i = norm_g[i]
        j = i // 2
        if i % 2 == 0:
            assert need_ctx
            cp = (w_pw1, conv_b_pw1[j][None, :], conv_w_dw[j], conv_b_dw[j][None, :],
                  conv_ln_g[j][None, :], conv_ln_b[j][None, :], w_pw2, conv_b_pw2[j][None, :])
            src_x, src_c = (x, ctx.reshape(1, BATCH * CTX_LEN, D_MODEL)) if i == 0 else (xs, xs)
            new = _conv(src_x, src_c, mod_i, g_i, *cp, layer=j, ctx=False)
            xs = _conv(src_c, new, mod_i, g_i, *cp, layer=j, ctx=True)
            xs = _ffn(xs, mod_i, g_i, w1, w3, w2, i, n_blk, in_place=need_ctx)
        else:
            qtab = _norm_rope_table(attn_q_g[j], cos_t, sin_t, LOG2_E * HEAD_DIM ** -0.5)
            ktab = _norm_rope_table(attn_k_g[j], cos_t, sin_t, 1.0)
            q, k, v = _qkv(xs, mod_i, g_i, wqkv_t, qtab, ktab, j)
            o = _attn_latent(q, k, v)
            if need_ctx:
                o = _attn_ctx(q, k, v, o)
            xs = _ffn(xs, mod_i, g_i, w1, w3, w2, i, n_blk, in_place=need_ctx, attn=(o, wo, j))
    return xs
```

```python
import functools

import numpy as np
import jax
import jax.numpy as jnp
from jax import lax
from jax.experimental import pallas as pl
from jax.experimental.pallas import tpu as pltpu

D_MODEL = 1024
BATCH = 8
SEQ = 2048
DEPTH = 4
CTX_LEN = 256
GRID_W = 64
N_HEADS = 16
N_KV_HEADS = 4
HEAD_DIM = 64
KV_GROUP = N_HEADS // N_KV_HEADS
ROPE_PAIRS_PER_AXIS = HEAD_DIM // 4
ROPE_THETA = 10000.0
CONV_WIDTH = 31
CONV_PAD = CONV_WIDTH // 2
D_FF = 2816
N_MOD = 6
EPS = 1e-6
LOG2_E = 1.4426950408889634

N_STACK = BATCH + 1
CTX_BLK = BATCH
KV_DIM = N_KV_HEADS * HEAD_DIM
QKV_DIM = D_MODEL + 2 * KV_DIM
LANES = 128
HALO = 16
MOD_ROWS = 16
V_ROWS = HEAD_DIM + 16
VMEM_LIMIT = 56 * 1024 * 1024

BF16 = jnp.bfloat16
F32 = jnp.float32


def _params(sem):
    return pltpu.CompilerParams(dimension_semantics=sem, vmem_limit_bytes=VMEM_LIMIT)


def _rms(x, g):
    return x * lax.rsqrt(jnp.mean(x * x, axis=-1, keepdims=True) + EPS) * g


def _sigmoid(x):
    return 1.0 / (1.0 + jnp.exp(-x))


def _resident(shape, layer):
    nd = len(shape)
    return pl.BlockSpec((None,) + shape, lambda *_: (layer,) + (0,) * nd,
                        pipeline_mode=pl.Buffered(1))


def _small(shape):
    nd = len(shape)
    return pl.BlockSpec(shape, lambda *_: (0,) * nd)


def _mod_kernel(cc_ref, w_ref, b_ref, o_ref):
    s = cc_ref[...]
    s = s * _sigmoid(s)
    o_ref[0] = jnp.dot(s.astype(BF16), w_ref[0].astype(BF16),
                       preferred_element_type=F32) + b_ref[0]


def _mod_table(cc, w_mod, b_mod):
    tn = 1536
    n_out = N_MOD * D_MODEL
    return pl.pallas_call(
        _mod_kernel,
        grid=(DEPTH, n_out // tn),
        in_specs=[
            pl.BlockSpec((MOD_ROWS, D_MODEL), lambda i, j: (0, 0)),
            pl.BlockSpec((1, D_MODEL, tn), lambda i, j: (i, 0, j)),
            pl.BlockSpec((1, 1, tn), lambda i, j: (i, 0, j)),
        ],
        out_specs=pl.BlockSpec((1, MOD_ROWS, tn), lambda i, j: (i, 0, j)),
        out_shape=jax.ShapeDtypeStruct((DEPTH, MOD_ROWS, n_out), F32),
        compiler_params=_params(("arbitrary", "arbitrary")),
        name="mod_table",
    )(cc, w_mod, b_mod.reshape(DEPTH, 1, n_out))


def _ffn_kernel(x_ref, mod_ref, g_ref, w1_ref, w3_ref, w2_ref, *rest, n_chunks, n_sub):
    *attn, o_ref = rest
    rows = x_ref.shape[1] // n_sub
    fc = D_FF // n_chunks
    xs = [x_ref[0, s * rows:(s + 1) * rows, :] for s in range(n_sub)]
    if attn:
        attn_ref, wo_ref = attn
        ys = [jnp.dot(attn_ref[0, s * rows:(s + 1) * rows, :], wo_ref[...],
                      preferred_element_type=F32) for s in range(n_sub)]
        xs = [x + mod_ref[0, 2:3, :] * _rms(y, g_ref[1:2, :]) for x, y in zip(xs, ys)]
    hb = [(_rms(x, g_ref[2:3, :]) * (1.0 + mod_ref[0, 4:5, :]) + mod_ref[0, 3:4, :]).astype(BF16)
          for x in xs]
    acc = [None] * n_sub
    for c in range(n_chunks):
        cols = slice(c * fc, (c + 1) * fc)
        ab = [(jnp.dot(h, w1_ref[:, cols], preferred_element_type=F32),
               jnp.dot(h, w3_ref[:, cols], preferred_element_type=F32)) for h in hb]
        for s, (a, b) in enumerate(ab):
            act = (a * _sigmoid(a) * b).astype(BF16)
            part = jnp.dot(act, w2_ref[cols, :], preferred_element_type=F32)
            acc[s] = part if acc[s] is None else acc[s] + part
    for s in range(n_sub):
        o_ref[0, s * rows:(s + 1) * rows, :] = (
            xs[s] + mod_ref[0, 5:6, :] * _rms(acc[s], g_ref[3:4, :]))


def _ffn(xs, mod_i, g_i, w1, w3, w2, layer, n_blk, in_place, attn=None):
    tm = 1024
    kern = functools.partial(_ffn_kernel, n_chunks=2, n_sub=4)
    out_rows = N_STACK if in_place else n_blk
    tile = pl.BlockSpec((1, tm, D_MODEL), lambda b, j: (b, j, 0))
    in_specs = [
        tile,
        pl.BlockSpec((1, N_MOD, D_MODEL), lambda b, j: (b, 0, 0)),
        _small((4, D_MODEL)),
        _resident((D_MODEL, D_FF), layer),
        _resident((D_MODEL, D_FF), layer),
        _resident((D_FF, D_MODEL), layer),
    ]
    args = [xs, mod_i, g_i, w1, w3, w2]
    if attn is not None:
        o, wo, attn_layer = attn
        in_specs += [tile, _resident((D_MODEL, D_MODEL), attn_layer)]
        args += [o, wo]
    return pl.pallas_call(
        kern,
        grid=(n_blk, SEQ // tm),
        in_specs=in_specs,
        out_specs=tile,
        out_shape=jax.ShapeDtypeStruct((out_rows, SEQ, D_MODEL), F32),
        input_output_aliases={0: 0} if in_place else {},
        compiler_params=_params(("arbitrary", "arbitrary")),
        name="ffn" if attn is None else "oproj_ffn",
    )(*args)


CONV_ROWS = 128


def _conv_kernel(x_ref, xp_ref, xn_ref, mod_ref, g_ref, w1_ref, b1_ref, wdw_ref, bdw_ref,
                 lng_ref, lnb_ref, w2_ref, b2_ref, o_ref, hb_ref, buf_ref, *, tm):
    j = pl.program_id(1)
    nj = pl.num_programs(1)
    win = tm + 2 * HALO
    n_grp = tm // CONV_ROWS

    def prenorm(x):
        h = _rms(x, g_ref[0:1, :]) * (1.0 + mod_ref[0, 1:2, :]) + mod_ref[0, 0:1, :]
        return h.astype(BF16)

    hb_ref[0:HALO, :] = prenorm(xp_ref[0])
    hb_ref[HALO:HALO + tm, :] = prenorm(x_ref[0])
    hb_ref[HALO + tm:win, :] = prenorm(xn_ref[0])

    def pointwise1(lo, hi):
        hb = hb_ref[lo:hi, :]
        a = jnp.dot(hb, w1_ref[:, :D_MODEL], preferred_element_type=F32) + b1_ref[:, :D_MODEL]
        gt = jnp.dot(hb, w1_ref[:, D_MODEL:], preferred_element_type=F32) + b1_ref[:, D_MODEL:]
        u = a * _sigmoid(gt)
        if lo < HALO or hi > HALO + tm:
            row = lax.broadcasted_iota(jnp.int32, (hi - lo, 1), 0) + lo
            valid = (((row >= HALO) | (j > 0)) & ((row < HALO + tm) | (j < nj - 1)))
            u = jnp.where(valid, u, 0.0)
        for c in range(D_MODEL // LANES):
            buf_ref[c, lo:hi, :] = u[:, c * LANES:(c + 1) * LANES]

    def conv_group(r):
        r0 = r * CONV_ROWS
        base = HALO - CONV_PAD
        cols = []
        for c in range(D_MODEL // LANES):
            lanes = slice(c * LANES, (c + 1) * LANES)
            acc = jnp.zeros((CONV_ROWS, LANES), F32)
            for k in range(CONV_WIDTH):
                lo = r0 + base + k
                acc = acc + buf_ref[c, lo:lo + CONV_ROWS, :] * wdw_ref[k:k + 1, lanes]
            cols.append(acc + bdw_ref[:, lanes])
        v = jnp.concatenate(cols, axis=1)
        mu = jnp.mean(v, axis=-1, keepdims=True)
        vc = v - mu
        var = jnp.mean(vc * vc, axis=-1, keepdims=True)
        y = vc * lax.rsqrt(var + EPS) * lng_ref[...] + lnb_ref[...]
        y = y * _sigmoid(y)
        z = jnp.dot(y.astype(BF16), w2_ref[...], preferred_element_type=F32) + b2_ref[...]
        rows = slice(r0, r0 + CONV_ROWS)
        o_ref[0, rows, :] = x_ref[0, rows, :] + mod_ref[0, 2:3, :] * _rms(z, g_ref[1:2, :])

    bounds = [0] + [(r + 1) * CONV_ROWS + 2 * HALO for r in range(n_grp)]
    pointwise1(bounds[0], bounds[1])
    for r in range(n_grp):
        if r + 1 < n_grp:
            pointwise1(bounds[r + 1], bounds[r + 2])
        conv_group(r)


def _conv_ctx_kernel(*refs, tm):
    _conv_kernel(*refs[:13], *refs[14:], tm=tm)


def _conv_latent_kernel(*refs, tm):
    b = pl.program_id(0)

    @pl.when(b < BATCH)
    def _():
        _conv_kernel(*refs[:13], *refs[14:], tm=tm)

    @pl.when(b == BATCH)
    def _():
        refs[14][0] = refs[13][0]


def _conv(src, dst, mod_i, g_i, w1, b1, wdw, bdw, lng, lnb, w2, b2, *, layer, ctx):
    if ctx:
        tm = CTX_LEN
        grid = (BATCH, 1)
        blk = src.shape[0] - 1
        tile = lambda b, j: (blk, b, 0)
        prev = lambda b, j: (blk, 0, 0)
        nxt = lambda b, j: (blk, 0, 0)
        mod_map = lambda b, j: (CTX_BLK, 0, 0)
        out_tile = lambda b, j: (CTX_BLK, b, 0)
    else:
        tm = 512
        grid = (N_STACK, SEQ // tm)
        hb = tm // HALO
        lat = lambda b: jnp.minimum(b, BATCH - 1)
        tile = lambda b, j: (lat(b), j, 0)
        prev = lambda b, j: (lat(b), jnp.maximum(j * hb - 1, 0), 0)
        nxt = lambda b, j: (lat(b), jnp.minimum((j + 1) * hb, SEQ // HALO - 1), 0)
        mod_map = lambda b, j: (lat(b), 0, 0)
        out_tile = lambda b, j: (b, j, 0)
        ctx_blk = dst.shape[0] - 1
        ctx_tile = lambda b, j: (ctx_blk, jnp.where(b == BATCH, j, 0), 0)
    in_specs = [
        pl.BlockSpec((1, tm, D_MODEL), tile),
        pl.BlockSpec((1, HALO, D_MODEL), prev),
        pl.BlockSpec((1, HALO, D_MODEL), nxt),
        pl.BlockSpec((1, N_MOD, D_MODEL), mod_map),
        _small((4, D_MODEL)),
        _resident((D_MODEL, 2 * D_MODEL), layer),
        _small((1, 2 * D_MODEL)),
        _small((CONV_WIDTH, D_MODEL)),
        _small((1, D_MODEL)),
        _small((1, D_MODEL)),
        _small((1, D_MODEL)),
        _resident((D_MODEL, D_MODEL), layer),
        _small((1, D_MODEL)),
    ]
    args = [src, src, src, mod_i, g_i, w1, b1, wdw, bdw, lng, lnb, w2, b2]
    if ctx:
        kern = functools.partial(_conv_ctx_kernel, tm=tm)
        in_specs.append(pl.BlockSpec(memory_space=pl.ANY))
        args.append(dst)
        aliases = {len(args) - 1: 0}
    else:
        kern = functools.partial(_conv_latent_kernel, tm=tm)
        in_specs.append(pl.BlockSpec((1, tm, D_MODEL), ctx_tile))
        args.append(dst)
        aliases = {}
    return pl.pallas_call(
        kern,
        grid=grid,
        in_specs=in_specs,
        out_specs=pl.BlockSpec((1, tm, D_MODEL), out_tile),
        out_shape=jax.ShapeDtypeStruct((N_STACK, SEQ, D_MODEL), F32),
        scratch_shapes=[
            pltpu.VMEM((tm + 2 * HALO, D_MODEL), BF16),
            pltpu.VMEM((D_MODEL // LANES, tm + 2 * HALO, LANES), F32),
        ],
        input_output_aliases=aliases,
        compiler_params=_params(("arbitrary", "arbitrary")),
        name="conv_ctx" if ctx else "conv",
    )(*args)


HALF = HEAD_DIM // 2


def _head_norm_rope_t(xh, tab):
    xe, xo = xh[:HALF], xh[HALF:]
    ss = jnp.sum(xe * xe, axis=0, keepdims=True) + jnp.sum(xo * xo, axis=0, keepdims=True)
    rs = lax.rsqrt(ss * (1.0 / HEAD_DIM) + EPS)
    oe = (xe * tab[0:HALF] + xo * tab[HALF:2 * HALF]) * rs
    oo = (xe * tab[2 * HALF:3 * HALF] + xo * tab[3 * HALF:4 * HALF]) * rs
    return oe, oo


def _qkv_kernel(x_ref, mod_ref, g_ref, wt_ref, qtab_ref, ktab_ref, q_ref, k_ref, v_ref):
    x = x_ref[0]
    h = _rms(x, g_ref[0:1, :]) * (1.0 + mod_ref[0, 1:2, :]) + mod_ref[0, 0:1, :]
    nt = (((1,), (1,)), ((), ()))
    yt = lax.dot_general(wt_ref[...], h.astype(BF16), nt,
                         preferred_element_type=F32)
    qtab = qtab_ref[0]
    ktab = ktab_ref[0]
    for hd in range(N_HEADS):
        oe, oo = _head_norm_rope_t(yt[hd * HEAD_DIM:(hd + 1) * HEAD_DIM], qtab)
        q_ref[0, hd, 0:HALF, :] = oe.astype(BF16)
        q_ref[0, hd, HALF:HEAD_DIM, :] = oo.astype(BF16)
    for hd in range(N_KV_HEADS):
        lo = D_MODEL + hd * HEAD_DIM
        oe, oo = _head_norm_rope_t(yt[lo:lo + HEAD_DIM], ktab)
        k_ref[0, hd] = jnp.concatenate([oe, oo], axis=0).T.astype(BF16)
    row = lax.broadcasted_iota(jnp.int32, (V_ROWS - HEAD_DIM, x.shape[0]), 0)
    tail = jnp.where(row == 0, 1.0, 0.0).astype(BF16)
    for hd in range(N_KV_HEADS):
        lo = D_MODEL + KV_DIM + hd * HEAD_DIM
        v_ref[0, hd, 0:HEAD_DIM, :] = yt[lo:lo + HEAD_DIM].astype(BF16)
        v_ref[0, hd, HEAD_DIM:V_ROWS, :] = tail


def _qkv(xs, mod_i, g_i, wqkv_t, qtab, ktab, layer):
    tm = 512
    tab_spec = pl.BlockSpec((1, 4 * HALF, tm), lambda b, j: (b // BATCH, 0, j))
    return pl.pallas_call(
        _qkv_kernel,
        grid=(N_STACK, SEQ // tm),
        in_specs=[
            pl.BlockSpec((1, tm, D_MODEL), lambda b, j: (b, j, 0)),
            pl.BlockSpec((1, N_MOD, D_MODEL), lambda b, j: (b, 0, 0)),
            _small((4, D_MODEL)),
            _resident((QKV_DIM, D_MODEL), layer),
            tab_spec,
            tab_spec,
        ],
        out_specs=[
            pl.BlockSpec((1, N_HEADS, HEAD_DIM, tm), lambda b, j: (b, 0, 0, j)),
            pl.BlockSpec((1, N_KV_HEADS, tm, HEAD_DIM), lambda b, j: (b, 0, j, 0)),
            pl.BlockSpec((1, N_KV_HEADS, V_ROWS, tm), lambda b, j: (b, 0, 0, j)),
        ],
        out_shape=[
            jax.ShapeDtypeStruct((N_STACK, N_HEADS, HEAD_DIM, SEQ), BF16),
            jax.ShapeDtypeStruct((N_STACK, N_KV_HEADS, SEQ, HEAD_DIM), BF16),
            jax.ShapeDtypeStruct((N_STACK, N_KV_HEADS, V_ROWS, SEQ), BF16),
        ],
        compiler_params=_params(("arbitrary", "arbitrary")),
        name="attn_qkv",
    )(xs, mod_i, g_i, wqkv_t, qtab, ktab)


Q_COLS = 256


def _attn_kernel(*refs, n_src, tk, lag):
    q_ref = refs[0]
    kv_refs = refs[1:1 + 2 * n_src]
    o_ref = refs[-1]
    chunks = []
    for i in range(n_src):
        k_ref, vt_ref = kv_refs[2 * i], kv_refs[2 * i + 1]
        n_keys = k_ref.shape[2]
        step = min(tk, n_keys)
        chunks += [(k_ref, vt_ref, lo, step) for lo in range(0, n_keys, step)]
    tq = q_ref.shape[3]
    groups = [(h, qs) for qs in range(0, tq, Q_COLS) for h in range(KV_GROUP)]
    items = [(c, g) for c in chunks for g in range(len(groups))]
    m = [None] * len(groups)
    acc = [None] * len(groups)
    scores = {}
    for idx in range(len(items) + lag):
        if idx < len(items):
            (k_ref, _, lo, step), g = items[idx]
            h, qs = groups[g]
            scores[idx] = jnp.dot(k_ref[0, 0, lo:lo + step, :], q_ref[0, h, :, qs:qs + Q_COLS],
                                  preferred_element_type=F32)
        if idx >= lag:
            (_, vt_ref, lo, step), g = items[idx - lag]
            st = scores.pop(idx - lag)
            mj = st.max(axis=0, keepdims=True)
            m_new = mj if m[g] is None else jnp.maximum(m[g], mj)
            pt = jnp.exp2(st - m_new).astype(BF16)
            pv = jnp.dot(vt_ref[0, 0, :, lo:lo + step], pt, preferred_element_type=F32)
            acc[g] = pv if acc[g] is None else jnp.exp2(m[g] - m_new) * acc[g] + pv
            m[g] = m_new
    for qs in range(0, tq, Q_COLS):
        outs = [acc[g][:HEAD_DIM] / acc[g][HEAD_DIM:HEAD_DIM + 1]
                for g, (_, gq) in enumerate(groups) if gq == qs]
        o_ref[0, qs:qs + Q_COLS, :] = jnp.concatenate(outs, axis=0).T.astype(BF16)


def _attn_latent_kernel(*refs, **kw):
    b = pl.program_id(0)

    @pl.when(b < BATCH)
    def _():
        _attn_kernel(*refs, **kw)

    @pl.when(b == BATCH)
    def _():
        refs[-1][...] = jnp.zeros(refs[-1].shape, refs[-1].dtype)


def _attn_latent(q, k, vt):
    tq = 2048
    kern = functools.partial(_attn_latent_kernel, n_src=2, tk=256, lag=6)
    seq = lambda b: jnp.minimum(b, BATCH - 1)
    k_lat = pl.BlockSpec((1, 1, SEQ, HEAD_DIM), lambda b, g, i: (b, g, 0, 0))
    k_ctx = pl.BlockSpec((1, 1, CTX_LEN, HEAD_DIM), lambda b, g, i: (CTX_BLK, g, seq(b), 0))
    vt_lat = pl.BlockSpec((1, 1, V_ROWS, SEQ), lambda b, g, i: (b, g, 0, 0))
    vt_ctx = pl.BlockSpec((1, 1, V_ROWS, CTX_LEN), lambda b, g, i: (CTX_BLK, g, 0, seq(b)))
    return pl.pallas_call(
        kern,
        grid=(N_STACK, N_KV_HEADS, SEQ // tq),
        in_specs=[
            pl.BlockSpec((1, KV_GROUP, HEAD_DIM, tq), lambda b, g, i: (b, g, 0, i)),
            k_lat, vt_lat, k_ctx, vt_ctx,
        ],
        out_specs=pl.BlockSpec((1, tq, KV_GROUP * HEAD_DIM), lambda b, g, i: (b, i, g)),
        out_shape=jax.ShapeDtypeStruct((N_STACK, SEQ, D_MODEL), BF16),
        compiler_params=_params(("arbitrary", "arbitrary", "arbitrary")),
        name="attn_latent",
    )(q, k, vt, k, vt)


def _attn_ctx_kernel(q_ref, k_ref, vt_ref, o_prev_ref, o_ref):
    del o_prev_ref
    _attn_kernel(q_ref, k_ref, vt_ref, o_ref, n_src=1, tk=CTX_LEN, lag=2)


def _attn_ctx(q, k, vt, o):
    return pl.pallas_call(
        _attn_ctx_kernel,
        grid=(BATCH, N_KV_HEADS),
        in_specs=[
            pl.BlockSpec((1, KV_GROUP, HEAD_DIM, CTX_LEN), lambda b, g: (CTX_BLK, g, 0, b)),
            pl.BlockSpec((1, 1, CTX_LEN, HEAD_DIM), lambda b, g: (CTX_BLK, g, b, 0)),
            pl.BlockSpec((1, 1, V_ROWS, CTX_LEN), lambda b, g: (CTX_BLK, g, 0, b)),
            pl.BlockSpec(memory_space=pl.ANY),
        ],
        out_specs=pl.BlockSpec((1, CTX_LEN, KV_GROUP * HEAD_DIM), lambda b, g: (CTX_BLK, b, g)),
        out_shape=jax.ShapeDtypeStruct((N_STACK, SEQ, D_MODEL), BF16),
        input_output_aliases={3: 0},
        compiler_params=_params(("arbitrary", "arbitrary")),
        name="attn_ctx",
    )(q, k, vt, o)


def _rope_tables():
    t = np.arange(SEQ)
    row = (t // GRID_W).astype(np.float64)
    col = (t % GRID_W).astype(np.float64)
    freqs = ROPE_THETA ** (-np.arange(ROPE_PAIRS_PER_AXIS, dtype=np.float64) / ROPE_PAIRS_PER_AXIS)
    ang = np.concatenate([row[:, None] * freqs[None, :], col[:, None] * freqs[None, :]], axis=-1)
    cos = np.cos(ang).T
    sin = np.sin(ang).T
    cos_t = np.stack([cos, np.ones_like(cos)]).astype(np.float32)
    sin_t = np.stack([sin, np.zeros_like(sin)]).astype(np.float32)
    return jnp.asarray(cos_t), jnp.asarray(sin_t)


def _norm_rope_table(g, cos_t, sin_t, scale):
    ge = g[0::2][None, :, None] * scale
    go = g[1::2][None, :, None] * scale
    return jnp.concatenate([ge * cos_t, -(go * sin_t), ge * sin_t, go * cos_t], axis=1)


def _even_odd_transposed(w, n_heads):
    n = w.shape[0]
    w = w.reshape(n, D_MODEL, n_heads, HALF, 2).transpose(0, 2, 4, 3, 1)
    return w.reshape(n, n_heads * HEAD_DIM, D_MODEL)


def kernel(x, c, ctx, c_ctx, w_mod, b_mod, norm_g, conv_w_pw1, conv_b_pw1, conv_w_dw, conv_b_dw, conv_ln_g, conv_ln_b, conv_w_pw2, conv_b_pw2, attn_wq, attn_wk, attn_wv, attn_wo, attn_q_g, attn_k_g, ffn_w1, ffn_w3, ffn_w2):
    assert x.shape == (BATCH, SEQ, D_MODEL) and ctx.shape == (BATCH, CTX_LEN, D_MODEL)
    xs = None
    cc = jnp.concatenate([c, c_ctx[None, :],
                          jnp.zeros((MOD_ROWS - N_STACK, D_MODEL), F32)], axis=0)
    mod = _mod_table(cc, w_mod, b_mod)[:, :N_STACK].reshape(DEPTH, N_STACK, N_MOD, D_MODEL)
    cos_t, sin_t = _rope_tables()

    w_pw1 = conv_w_pw1.astype(BF16)
    w_pw2 = conv_w_pw2.astype(BF16)
    wqkv_t = jnp.concatenate([_even_odd_transposed(attn_wq, N_HEADS),
                              _even_odd_transposed(attn_wk, N_KV_HEADS),
                              attn_wv.transpose(0, 2, 1)], axis=1).astype(BF16)
    wo = attn_wo.astype(BF16)
    w1 = ffn_w1.astype(BF16)
    w3 = ffn_w3.astype(BF16)
    w2 = ffn_w2.astype(BF16)

    for i in range(DEPTH):
        need_ctx = i < DEPTH - 1
        n_blk = N_STACK if need_ctx else BATCH
        mod_i = mod[i]
        g_i = norm_g[i]
        j = i // 2
        if i % 2 == 0:
            assert need_ctx
            cp = (w_pw1, conv_b_pw1[j][None, :], conv_w_dw[j], conv_b_dw[j][None, :],
                  conv_ln_g[j][None, :], conv_ln_b[j][None, :], w_pw2, conv_b_pw2[j][None, :])
            src_x, src_c = (x, ctx.reshape(1, BATCH * CTX_LEN, D_MODEL)) if i == 0 else (xs, xs)
            new = _conv(src_x, src_c, mod_i, g_i, *cp, layer=j, ctx=False)
            xs = _conv(src_c, new, mod_i, g_i, *cp, layer=j, ctx=True)
            xs = _ffn(xs, mod_i, g_i, w1, w3, w2, i, n_blk, in_place=need_ctx)
        else:
            qtab = _norm_rope_table(attn_q_g[j], cos_t, sin_t, LOG2_E * HEAD_DIM ** -0.5)
            ktab = _norm_rope_table(attn_k_g[j], cos_t, sin_t, 1.0)
            q, k, v = _qkv(xs, mod_i, g_i, wqkv_t, qtab, ktab, j)
            o = _attn_latent(q, k, v)
            if need_ctx:
                o = _attn_ctx(q, k, v, o)
            xs = _ffn(xs, mod_i, g_i, w1, w3, w2, i, n_blk, in_place=need_ctx, attn=(o, wo, j))
    return xs
```
